```python
import jax, jax.numpy as jnp
from jax import lax
import numpy as np

D_MODEL = 2048
BATCH = 2
SEQ = 8192
DEPTH = 2

N_META = 16
N_MIXERS = 2
CONV_WIDTH = 3
HEAD_DIM = 64
N_Q_HEADS = D_MODEL // HEAD_DIM
N_KV_HEADS = N_Q_HEADS // 8
GROUP = N_Q_HEADS // N_KV_HEADS
WINDOW = 128
BLOCK = 128
ROPE_THETA = 10000.0
D_FF = 4 * D_MODEL
RMS_EPS = 1e-5
NEG_INF = -1e30

kernel_name = "hybrid_shortconv_swa_sink_block"


def rms_norm(x, g):
    xf = x.astype(jnp.float32)
    var = jnp.mean(xf * xf, axis=-1, keepdims=True)
    return (xf * lax.rsqrt(var + RMS_EPS)).astype(x.dtype) * g


def short_conv_mixer(h, w_in, conv_w, w_out):
    bcu = h @ w_in
    b_gate, c_gate, u = jnp.split(bcu, 3, axis=-1)
    v = c_gate * u
    L = v.shape[1]
    vp = jnp.pad(v, ((0, 0), (CONV_WIDTH - 1, 0), (0, 0)))
    conv = conv_w[0] * vp[:, 0:L]
    for k in range(1, CONV_WIDTH):
        conv = conv + conv_w[k] * vp[:, k:k + L]
    return (b_gate * conv) @ w_out


def rope_tables(n_pos, offset):
    pos = jnp.arange(n_pos, dtype=jnp.float32) - offset
    inv = ROPE_THETA ** (-jnp.arange(0, HEAD_DIM, 2, dtype=jnp.float32) / HEAD_DIM)
    ang = pos[:, None] * inv[None, :]
    return jnp.cos(ang), jnp.sin(ang)


def apply_rope(x, cos, sin):
    x1, x2 = jnp.split(x, 2, axis=-1)
    c = cos[None, :, None, :]
    s = sin[None, :, None, :]
    return jnp.concatenate([x1 * c - x2 * s, x2 * c + x1 * s], axis=-1).astype(x.dtype)


def swa_sink_mixer(h, w_qkv, sinks, w_o):
    Bsz, L, _ = h.shape
    pad = (-L) % BLOCK
    P = L + pad
    nb = P // BLOCK
    hp = jnp.pad(h, ((0, 0), (pad, 0), (0, 0)))
    qkv = hp @ w_qkv
    q, k, v = jnp.split(qkv, [N_Q_HEADS * HEAD_DIM, (N_Q_HEADS + N_KV_HEADS) * HEAD_DIM], axis=-1)
    q = q.reshape(Bsz, P, N_Q_HEADS, HEAD_DIM)
    k = k.reshape(Bsz, P, N_KV_HEADS, HEAD_DIM)
    v = v.reshape(Bsz, P, N_KV_HEADS, HEAD_DIM)
    cos, sin = rope_tables(P, pad)
    q = apply_rope(q, cos, sin)
    k = apply_rope(k, cos, sin)

    qb = q.reshape(Bsz, nb, BLOCK, N_KV_HEADS, GROUP, HEAD_DIM)
    kb = k.reshape(Bsz, nb, BLOCK, N_KV_HEADS, HEAD_DIM)
    vb = v.reshape(Bsz, nb, BLOCK, N_KV_HEADS, HEAD_DIM)
    zpad = ((0, 0), (1, 0), (0, 0), (0, 0), (0, 0))
    k_band = jnp.concatenate([jnp.pad(kb[:, :-1], zpad), kb], axis=2)
    v_band = jnp.concatenate([jnp.pad(vb[:, :-1], zpad), vb], axis=2)

    scale = HEAD_DIM ** -0.5
    s = jnp.einsum("bnqhgd,bnkhd->bhgnqk", qb, k_band).astype(jnp.float32) * scale

    blk = jnp.arange(nb)[:, None, None] * BLOCK
    q_idx = blk + jnp.arange(BLOCK)[None, :, None]
    k_idx = blk - BLOCK + jnp.arange(2 * BLOCK)[None, None, :]
    diff = q_idx - k_idx
    allowed = (diff >= 0) & (diff < WINDOW) & (k_idx >= pad)
    s = jnp.where(allowed[None, None, None], s, NEG_INF)

    sink = sinks.astype(jnp.float32).reshape(N_KV_HEADS, GROUP)[None, :, :, None, None, None]
    m = jnp.maximum(jnp.max(s, axis=-1, keepdims=True), sink)
    e = jnp.exp(s - m)
    den = jnp.sum(e, axis=-1, keepdims=True) + jnp.exp(sink - m)
    p = (e / den).astype(v.dtype)

    o = jnp.einsum("bhgnqk,bnkhd->bnqhgd", p, v_band).reshape(Bsz, P, N_Q_HEADS * HEAD_DIM)
    return o[:, pad:] @ w_o


def squared_relu_mlp(h, w_up, w_down):
    a = jax.nn.relu(h @ w_up)
    return (a * a) @ w_down


def setup_inputs(seed: int = 0) -> dict:
    key = jax.random.key(seed)
    ks = jax.random.split(key, 20)
    D = D_MODEL
    f32 = jnp.float32

    def nrm(k, shape, scale):
        return jax.random.normal(k, shape, f32) * scale

    def gain(k):
        return jnp.ones((D,), f32) + 0.02 * jax.random.normal(k, (D,), f32)

    return {
        "x": nrm(ks[0], (BATCH, SEQ, D), 1.0),
        "meta_tokens": nrm(ks[1], (N_META, D), 1.0),
        "norm_mix_0": gain(ks[2]),
        "w_in_conv": nrm(ks[3], (D, 3 * D), D ** -0.5),
        "conv_w": nrm(ks[4], (CONV_WIDTH, D), CONV_WIDTH ** -0.5),
        "w_out_conv": nrm(ks[5], (D, D), D ** -0.5),
        "norm_mlp_0": gain(ks[6]),
        "w_up_0": nrm(ks[7], (D, D_FF), D ** -0.5),
        "w_down_0": nrm(ks[8], (D_FF, D), D_FF ** -0.5),
        "norm_mix_1": gain(ks[9]),
        "w_qkv": nrm(ks[10], (D, (N_Q_HEADS + 2 * N_KV_HEADS) * HEAD_DIM), D ** -0.5),
        "attn_sinks": nrm(ks[11], (N_Q_HEADS,), 0.5),
        "w_o": nrm(ks[12], (N_Q_HEADS * HEAD_DIM, D), (N_Q_HEADS * HEAD_DIM) ** -0.5),
        "norm_mlp_1": gain(ks[13]),
        "w_up_1": nrm(ks[14], (D, D_FF), D ** -0.5),
        "w_down_1": nrm(ks[15], (D_FF, D), D_FF ** -0.5),
        "norm_final": gain(ks[16]),
    }


def reference(x, meta_tokens, norm_mix_0, w_in_conv, conv_w, w_out_conv, norm_mlp_0, w_up_0, w_down_0,
              norm_mix_1, w_qkv, attn_sinks, w_o, norm_mlp_1, w_up_1, w_down_1, norm_final):
    Bsz = x.shape[0]
    meta = jnp.broadcast_to(meta_tokens[None].astype(x.dtype), (Bsz, N_META, D_MODEL))
    h = jnp.concatenate([meta, x], axis=1)

    mixers = [
        lambda t: short_conv_mixer(t, w_in_conv, conv_w, w_out_conv),
        lambda t: swa_sink_mixer(t, w_qkv, attn_sinks, w_o),
    ]
    mix_norms = [norm_mix_0, norm_mix_1]
    mlps = [(norm_mlp_0, w_up_0, w_down_0), (norm_mlp_1, w_up_1, w_down_1)]

    for i in range(DEPTH):
        h = h + mixers[i % N_MIXERS](rms_norm(h, mix_norms[i]))
        g, wu, wd = mlps[i]
        h = h + squared_relu_mlp(rms_norm(h, g), wu, wd)

    out = rms_norm(h, norm_final)
    return out[:, N_META:]
```

```python
import functools

import jax
import jax.numpy as jnp
from jax import lax
from jax.experimental import pallas as pl
from jax.experimental.pallas import tpu as pltpu

HEAD_DIM = 64
HALF_DIM = HEAD_DIM // 2
ATTN_BLOCK = 128
ROPE_THETA = 10000.0
RMS_EPS = 1e-5
NEG_INF = -1e30
CONV_WIDTH = 3

V7X_LANES = 128
V7X_SUBLANES = 8
V7X_VMEM_LIMIT_BYTES = 60000 * 1024

TOKEN_TILE = 512
COL_CHUNK = 512
FF_CHUNK = 512

BF16 = jnp.bfloat16
F32 = jnp.float32


def _dot(a, b):
    return jnp.dot(a, b, preferred_element_type=F32)


def _rms_norm(x, g):
    var = jnp.mean(x * x, axis=-1, keepdims=True)
    return (x * lax.rsqrt(var + RMS_EPS)) * g


def _params(est_bytes):
    limit = min(V7X_VMEM_LIMIT_BYTES, max(int(est_bytes * 1.25), 16 * 1024 * 1024))
    return pltpu.CompilerParams(
        dimension_semantics=("arbitrary", "arbitrary"), vmem_limit_bytes=limit)


def _conv_mixer_kernel(x_ref, g_ref, wb_ref, wc_ref, wu_ref, cw_ref, wo_ref, tail_in_ref,
                       o_ref, tail_out_ref, hn_ref, carry_ref, *, tiles_per_seq):
    i = pl.program_id(0)
    j = pl.program_id(1)

    @pl.when(j == 0)
    def _():
        x = x_ref[...]
        hn_ref[...] = _rms_norm(x, g_ref[...]).astype(BF16)
        o_ref[...] = x

    hn = hn_ref[...]
    gate_b = _dot(hn, wb_ref[...])
    v = _dot(hn, wc_ref[...]) * _dot(hn, wu_ref[...])
    tm = v.shape[0]

    first_tile_of_seq = (i % tiles_per_seq) == 0
    prev = jnp.where(first_tile_of_seq, tail_in_ref[...], carry_ref[j])
    vcat = jnp.concatenate([prev, v], axis=0)
    v_m1 = pltpu.roll(vcat, 1, axis=0)[V7X_SUBLANES:]
    v_m2 = pltpu.roll(vcat, 2, axis=0)[V7X_SUBLANES:]
    cw = cw_ref[...]
    conv = cw[0:1] * v_m2 + cw[1:2] * v_m1 + cw[2:3] * v
    gated = (gate_b * conv).astype(BF16)
    o_ref[...] += _dot(gated, wo_ref[...])

    tail = v[tm - V7X_SUBLANES:]
    carry_ref[j] = tail
    tail_out_ref[...] = tail


def _conv_mixer(x, gain, w_in, conv_w, w_out, tail_in, *, tm, nc, tiles_per_seq):
    t, d = x.shape
    nt, nj = t // tm, d // nc
    est = (2 * 2 * tm * d * 4 + tm * d * 2 + 2 * 4 * d * nc * 2 + 8 * tm * nc * 4)
    kern = functools.partial(_conv_mixer_kernel, tiles_per_seq=tiles_per_seq)
    return pl.pallas_call(
        kern,
        grid=(nt, nj),
        in_specs=[
            pl.BlockSpec((tm, d), lambda i, j: (i, 0)),
            pl.BlockSpec((1, d), lambda i, j: (0, 0)),
            pl.BlockSpec((d, nc), lambda i, j: (0, j)),
            pl.BlockSpec((d, nc), lambda i, j: (0, nj + j)),
            pl.BlockSpec((d, nc), lambda i, j: (0, 2 * nj + j)),
            pl.BlockSpec((CONV_WIDTH, nc), lambda i, j: (0, j)),
            pl.BlockSpec((nc, d), lambda i, j: (j, 0)),
            pl.BlockSpec((V7X_SUBLANES, nc), lambda i, j: (0, j)),
        ],
        out_specs=[
            pl.BlockSpec((tm, d), lambda i, j: (i, 0)),
            pl.BlockSpec((V7X_SUBLANES, nc), lambda i, j: (i, j)),
        ],
        out_shape=[
            jax.ShapeDtypeStruct((t, d), F32),
            jax.ShapeDtypeStruct((nt * V7X_SUBLANES, d), F32),
        ],
        scratch_shapes=[
            pltpu.VMEM((tm, d), BF16),
            pltpu.VMEM((nj, V7X_SUBLANES, nc), F32),
        ],
        compiler_params=_params(est),
        name="conv_mixer",
    )(x, gain, w_in, w_in, w_in, conv_w, w_out, tail_in)


def _mlp_kernel(x_ref, g_ref, wu_ref, wd_ref, gf_ref, o_ref, hn_ref, *, final_norm):
    j = pl.program_id(1)

    @pl.when(j == 0)
    def _():
        x = x_ref[...]
        hn_ref[...] = _rms_norm(x, g_ref[...]).astype(BF16)
        o_ref[...] = x

    a = jnp.maximum(_dot(hn_ref[...], wu_ref[...]), 0.0)
    o_ref[...] += _dot((a * a).astype(BF16), wd_ref[...])

    if final_norm:
        @pl.when(j == pl.num_programs(1) - 1)
        def _():
            o_ref[...] = _rms_norm(o_ref[...], gf_ref[...])


def _mlp(x, gain, w_up, w_down, gain_final, *, tm, fc, final_norm):
    t, d = x.shape
    f = w_up.shape[1]
    est = 2 * 2 * tm * d * 4 + tm * d * 2 + 2 * 2 * d * fc * 2 + 3 * tm * fc * 4
    return pl.pallas_call(
        functools.partial(_mlp_kernel, final_norm=final_norm),
        grid=(t // tm, f // fc),
        in_specs=[
            pl.BlockSpec((tm, d), lambda i, j: (i, 0)),
            pl.BlockSpec((1, d), lambda i, j: (0, 0)),
            pl.BlockSpec((d, fc), lambda i, j: (0, j)),
            pl.BlockSpec((fc, d), lambda i, j: (j, 0)),
            pl.BlockSpec((1, d), lambda i, j: (0, 0)),
        ],
        out_specs=pl.BlockSpec((tm, d), lambda i, j: (i, 0)),
        out_shape=jax.ShapeDtypeStruct((t, d), F32),
        scratch_shapes=[pltpu.VMEM((tm, d), BF16)],
        compiler_params=_params(est),
        name="mlp_final" if final_norm else "mlp",
    )(x, gain, w_up, w_down, gain_final)


def _qkv_kernel(x_ref, g_ref, w_ref, cos_ref, sin_ref, q_ref, kv_ref, hn_ref, *, n_q_chunks, scale):
    j = pl.program_id(1)

    @pl.when(j == 0)
    def _():
        hn_ref[...] = _rms_norm(x_ref[...], g_ref[...]).astype(BF16)

    y = _dot(hn_ref[...], w_ref[...])
    nslab = y.shape[1] // V7X_LANES
    lane = lax.broadcasted_iota(jnp.int32, (1, V7X_LANES), 1)
    first_half = (lane % HEAD_DIM) < HALF_DIM
    low_head = lane < HEAD_DIM

    def rope(t):
        rot = jnp.where(first_half,
                        pltpu.roll(t, V7X_LANES - HALF_DIM, axis=1),
                        pltpu.roll(t, HALF_DIM, axis=1))
        return t * cos_ref[...] + rot * sin_ref[...]

    def duplicate_heads(t):
        swapped = pltpu.roll(t, HEAD_DIM, axis=1)
        return jnp.where(low_head, t, swapped), jnp.where(low_head, swapped, t)

    @pl.when(j < n_q_chunks)
    def _():
        for c in range(nslab):
            sl = slice(c * V7X_LANES, (c + 1) * V7X_LANES)
            q_ref[:, sl] = (rope(y[:, sl]) * scale).astype(BF16)

    @pl.when(j == n_q_chunks)
    def _():
        half = nslab // 2
        for c in range(nslab):
            t = y[:, c * V7X_LANES:(c + 1) * V7X_LANES]
            if c < half:
                t = rope(t)
            a, b = duplicate_heads(t)
            kv_ref[:, (2 * c) * V7X_LANES:(2 * c + 1) * V7X_LANES] = a.astype(BF16)
            kv_ref[:, (2 * c + 1) * V7X_LANES:(2 * c + 2) * V7X_LANES] = b.astype(BF16)


def _qkv(x, gain, w_qkv, cos_t, sin_t, *, tm, nc, tiles_per_seq):
    t, d = x.shape
    n_q_chunks = d // nc
    assert w_qkv.shape[1] == d + nc, "k and v together must fill exactly one column chunk"
    est = 2 * tm * d * 4 + tm * d * 2 + 2 * d * nc * 2 + 2 * 2 * tm * 128 * 4 + 2 * 3 * tm * nc * 2 + 4 * tm * nc * 4
    kern = functools.partial(_qkv_kernel, n_q_chunks=n_q_chunks, scale=HEAD_DIM ** -0.5)
    return pl.pallas_call(
        kern,
        grid=(t // tm, n_q_chunks + 1),
        in_specs=[
            pl.BlockSpec((tm, d), lambda i, j: (i, 0)),
            pl.BlockSpec((1, d), lambda i, j: (0, 0)),
            pl.BlockSpec((d, nc), lambda i, j: (0, j)),
            pl.BlockSpec((tm, V7X_LANES), lambda i, j: (i % tiles_per_seq, 0)),
            pl.BlockSpec((tm, V7X_LANES), lambda i, j: (i % tiles_per_seq, 0)),
        ],
        out_specs=[
            pl.BlockSpec((tm, nc), lambda i, j: (i, jnp.minimum(j, n_q_chunks - 1))),
            pl.BlockSpec((tm, 2 * nc), lambda i, j: (i, 0)),
        ],
        out_shape=[
            jax.ShapeDtypeStruct((t, d), BF16),
            jax.ShapeDtypeStruct((t, 2 * nc), BF16),
        ],
        scratch_shapes=[pltpu.VMEM((tm, d), BF16)],
        compiler_params=_params(est),
        name="qkv_rope",
    )(x, gain, w_qkv, cos_t, sin_t)


def _attn_kernel(sink_ref, q_ref, kvc_ref, kvp_ref, kvm_ref, o_ref, *, n_kv, pad):
    n = pl.program_id(1)
    blk = ATTN_BLOCK
    kv_prev = jnp.where(n == 0, kvm_ref[...], kvp_ref[...])
    band = jnp.concatenate([kv_prev, kvc_ref[...]], axis=0)

    qi = lax.broadcasted_iota(jnp.int32, (blk, 2 * blk), 0)
    kj = lax.broadcasted_iota(jnp.int32, (blk, 2 * blk), 1)
    allowed = (kj > qi) & (kj <= qi + blk) & (kj + n * blk >= pad)

    low = lax.broadcasted_iota(jnp.int32, (2 * blk, V7X_LANES), 1) < HEAD_DIM
    zero = jnp.zeros((), BF16)
    pairs = 4
    contract_lanes = (((1,), (1,)), ((), ()))

    for h in range(n_kv):
        kd = band[:, h * V7X_LANES:(h + 1) * V7X_LANES]
        vd = band[:, (n_kv + h) * V7X_LANES:(n_kv + h + 1) * V7X_LANES]
        k_sides = (jnp.where(low, kd, zero), jnp.where(low, zero, kd))
        v_sides = (jnp.where(low, vd, zero), jnp.where(low, zero, vd))
        qbase = h * pairs * V7X_LANES
        qst = jnp.concatenate(
            [q_ref[:, qbase + p * V7X_LANES: qbase + (p + 1) * V7X_LANES] for p in range(pairs)],
            axis=0)
        acc = None
        for side in range(2):
            s_all = lax.dot_general(qst, k_sides[side], contract_lanes, preferred_element_type=F32)
            es, rs = [], []
            for p in range(pairs):
                s = jnp.where(allowed, s_all[p * blk:(p + 1) * blk], NEG_INF)
                sink = sink_ref[h * 2 * pairs + 2 * p + side]
                m = jnp.maximum(jnp.max(s, axis=-1, keepdims=True), sink)
                e = jnp.exp(s - m)
                den = jnp.sum(e, axis=-1, keepdims=True) + jnp.exp(sink - m)
                es.append(e.astype(BF16))
                rs.append(1.0 / den)
            e_all = jnp.concatenate(es, axis=0)
            r_all = jnp.concatenate(rs, axis=0)
            part = _dot(e_all, v_sides[side]) * r_all
            acc = part if acc is None else acc + part
        for p in range(pairs):
            o_ref[:, qbase + p * V7X_LANES: qbase + (p + 1) * V7X_LANES] = (
                acc[p * blk:(p + 1) * blk].astype(BF16))


def _attention(sinks, q, kv, kv_meta, *, batch, n_kv, pad):
    t, d = q.shape
    blk = ATTN_BLOCK
    nb = t // batch // blk
    kvw = kv.shape[1]
    est = 2 * 2 * blk * d * 2 + 3 * 2 * blk * kvw * 2 + 24 * 512 * 256 * 4
    return pl.pallas_call(
        functools.partial(_attn_kernel, n_kv=n_kv, pad=pad),
        grid=(batch, nb),
        in_specs=[
            pl.BlockSpec(memory_space=pltpu.SMEM),
            pl.BlockSpec((blk, d), lambda b, n: (b * nb + n, 0)),
            pl.BlockSpec((blk, kvw), lambda b, n: (b * nb + n, 0)),
            pl.BlockSpec((blk, kvw), lambda b, n: (b * nb + jnp.maximum(n - 1, 0), 0)),
            pl.BlockSpec((blk, kvw), lambda b, n: (0, 0)),
        ],
        out_specs=pl.BlockSpec((blk, d), lambda b, n: (b * nb + n, 0)),
        out_shape=jax.ShapeDtypeStruct((t, d), BF16),
        compiler_params=_params(est),
        name="swa_attention",
    )(sinks, q, kv, kv, kv_meta)


def _out_proj_kernel(a_ref, w_ref, h_ref, o_ref):
    o_ref[...] = h_ref[...] + _dot(a_ref[...], w_ref[...])


def _out_proj(a, w, h, *, tm, nc):
    t, d = h.shape
    k = a.shape[1]
    est = 2 * tm * k * 2 + 2 * k * nc * 2 + 2 * 2 * tm * nc * 4 + tm * nc * 4
    return pl.pallas_call(
        _out_proj_kernel,
        grid=(t // tm, d // nc),
        in_specs=[
            pl.BlockSpec((tm, k), lambda i, j: (i, 0)),
            pl.BlockSpec((k, nc), lambda i, j: (0, j)),
            pl.BlockSpec((tm, nc), lambda i, j: (i, j)),
        ],
        out_specs=pl.BlockSpec((tm, nc), lambda i, j: (i, j)),
        out_shape=jax.ShapeDtypeStruct((t, d), F32),
        compiler_params=_params(est),
        name="attn_out_proj",
    )(a, w, h)


def _rope_tables(n_pos):
    pos = jnp.arange(n_pos, dtype=F32)
    inv = ROPE_THETA ** (-jnp.arange(0, HEAD_DIM, 2, dtype=F32) / HEAD_DIM)
    ang = pos[:, None] * inv[None, :]
    cos, sin = jnp.cos(ang), jnp.sin(ang)
    return (jnp.concatenate([cos, cos, cos, cos], axis=1),
            jnp.concatenate([-sin, sin, -sin, sin], axis=1))


def kernel(x, meta_tokens, norm_mix_0, w_in_conv, conv_w, w_out_conv, norm_mlp_0, w_up_0, w_down_0,
           norm_mix_1, w_qkv, attn_sinks, w_o, norm_mlp_1, w_up_1, w_down_1, norm_final):
    batch, seq, d = x.shape
    n_meta = meta_tokens.shape[0]
    n_kv = (w_qkv.shape[1] - d) // (2 * HEAD_DIM)
    pad = (-(seq + n_meta)) % ATTN_BLOCK
    assert seq % ATTN_BLOCK == 0 and pad + n_meta == ATTN_BLOCK
    assert n_meta % (2 * V7X_SUBLANES) == 0 and d // HEAD_DIM == 8 * n_kv

    tm, nc, fc = TOKEN_TILE, COL_CHUNK, FF_CHUNK
    assert seq % tm == 0
    tiles_per_seq = seq // tm

    row = lambda g: g.reshape(1, d)
    g_mix0, g_mlp0, g_mix1, g_mlp1, g_fin = map(row, (norm_mix_0, norm_mlp_0, norm_mix_1, norm_mlp_1, norm_final))
    w_in, w_out, wu0, wd0, wqkv, wo, wu1, wd1 = (
        w.astype(BF16) for w in (w_in_conv, w_out_conv, w_up_0, w_down_0, w_qkv, w_o, w_up_1, w_down_1))
    cos_t, sin_t = _rope_tables(n_meta + seq)

    zeros_tail = jnp.zeros((V7X_SUBLANES, d), F32)
    hm, tail_m = _conv_mixer(meta_tokens, g_mix0, w_in, conv_w, w_out, zeros_tail,
                             tm=n_meta, nc=nc, tiles_per_seq=1)
    hm = _mlp(hm, g_mlp0, wu0, wd0, g_fin, tm=n_meta, fc=fc, final_norm=False)
    _, kv_m = _qkv(hm, g_mix1, wqkv, cos_t[:n_meta], sin_t[:n_meta], tm=n_meta, nc=nc, tiles_per_seq=1)
    kv_meta = jnp.pad(kv_m, ((pad, 0), (0, 0)))

    h = x.reshape(batch * seq, d)
    h, _ = _conv_mixer(h, g_mix0, w_in, conv_w, w_out, tail_m[-V7X_SUBLANES:],
                       tm=tm, nc=nc, tiles_per_seq=tiles_per_seq)
    h = _mlp(h, g_mlp0, wu0, wd0, g_fin, tm=tm, fc=fc, final_norm=False)
    q, kv = _qkv(h, g_mix1, wqkv, cos_t[n_meta:], sin_t[n_meta:], tm=tm, nc=nc, tiles_per_seq=tiles_per_seq)
    a = _attention(attn_sinks, q, kv, kv_meta, batch=batch, n_kv=n_kv, pad=pad)
    h = _out_proj(a, wo, h, tm=tm, nc=nc)
    out = _mlp(h, g_mlp1, wu1, wd1, g_fin, tm=tm, fc=fc, final_norm=True)
    return out.reshape(batch, seq, d)
```

```python
import functools

import jax
import jax.numpy as jnp
from jax import lax
from jax.experimental import pallas as pl
from jax.experimental.pallas import tpu as pltpu

HEAD_DIM = 64
HALF_DIM = HEAD_DIM // 2
ATTN_BLOCK = 128
ROPE_THETA = 10000.0
RMS_EPS = 1e-5
NEG_INF = -1e30
CONV_WIDTH = 3

V7X_LANES = 128
V7X_SUBLANES = 8
V7X_VMEM_LIMIT_BYTES = 60000 * 1024

TOKEN_TILE = 1024
CONV_TILE = 512
OUT_PROJ_TILE = 512
CONV_CHUNK = 512
COL_CHUNK = 512
FF_CHUNK = 512

BF16 = jnp.bfloat16
F32 = jnp.float32


def _dot(a, b):
    return jnp.dot(a, b, preferred_element_type=F32)


def _rms_norm(x, g):
    var = jnp.mean(x * x, axis=-1, keepdims=True)
    return (x * lax.rsqrt(var + RMS_EPS)) * g


def _params(est_bytes):
    limit = min(V7X_VMEM_LIMIT_BYTES, max(int(est_bytes * 1.25), 16 * 1024 * 1024))
    return pltpu.CompilerParams(
        dimension_semantics=("arbitrary", "arbitrary"), vmem_limit_bytes=limit)


def _conv_mixer_kernel(x_ref, g_ref, wb_ref, wc_ref, wu_ref, cw_ref, wo_ref, tail_in_ref,
                       o_ref, tail_out_ref, hn_ref, carry_ref, *, tiles_per_seq):
    i = pl.program_id(0)
    j = pl.program_id(1)

    @pl.when(j == 0)
    def _():
        x = x_ref[...]
        hn_ref[...] = _rms_norm(x, g_ref[...]).astype(BF16)
        o_ref[...] = x

    hn = hn_ref[...]
    gate_b = _dot(hn, wb_ref[...])
    v = _dot(hn, wc_ref[...]) * _dot(hn, wu_ref[...])
    tm = v.shape[0]

    first_tile_of_seq = (i % tiles_per_seq) == 0
    prev = jnp.where(first_tile_of_seq, tail_in_ref[...], carry_ref[j])
    vcat = jnp.concatenate([prev, v], axis=0)
    v_m1 = pltpu.roll(vcat, 1, axis=0)[V7X_SUBLANES:]
    v_m2 = pltpu.roll(vcat, 2, axis=0)[V7X_SUBLANES:]
    cw = cw_ref[...]
    conv = cw[0:1] * v_m2 + cw[1:2] * v_m1 + cw[2:3] * v
    gated = (gate_b * conv).astype(BF16)
    o_ref[...] += _dot(gated, wo_ref[...])

    tail = v[tm - V7X_SUBLANES:]
    carry_ref[j] = tail
    tail_out_ref[...] = tail


def _conv_mixer(x, gain, w_in, conv_w, w_out, tail_in, *, tm, nc, tiles_per_seq):
    t, d = x.shape
    nt, nj = t // tm, d // nc
    est = (2 * 2 * tm * d * 4 + tm * d * 2 + 2 * 4 * d * nc * 2 + 8 * tm * nc * 4)
    kern = functools.partial(_conv_mixer_kernel, tiles_per_seq=tiles_per_seq)
    return pl.pallas_call(
        kern,
        grid=(nt, nj),
        in_specs=[
            pl.BlockSpec((tm, d), lambda i, j: (i, 0)),
            pl.BlockSpec((1, d), lambda i, j: (0, 0)),
            pl.BlockSpec((d, nc), lambda i, j: (0, j)),
            pl.BlockSpec((d, nc), lambda i, j: (0, nj + j)),
            pl.BlockSpec((d, nc), lambda i, j: (0, 2 * nj + j)),
            pl.BlockSpec((CONV_WIDTH, nc), lambda i, j: (0, j)),
            pl.BlockSpec((nc, d), lambda i, j: (j, 0)),
            pl.BlockSpec((V7X_SUBLANES, nc), lambda i, j: (0, j)),
        ],
        out_specs=[
            pl.BlockSpec((tm, d), lambda i, j: (i, 0)),
            pl.BlockSpec((V7X_SUBLANES, nc), lambda i, j: (i, j)),
        ],
        out_shape=[
            jax.ShapeDtypeStruct((t, d), F32),
            jax.ShapeDtypeStruct((nt * V7X_SUBLANES, d), F32),
        ],
        scratch_shapes=[
            pltpu.VMEM((tm, d), BF16),
            pltpu.VMEM((nj, V7X_SUBLANES, nc), F32),
        ],
        compiler_params=_params(est),
        name="conv_mixer",
    )(x, gain, w_in, w_in, w_in, conv_w, w_out, tail_in)


def _mlp_kernel(x_ref, g_ref, wu_ref, wd_ref, gf_ref, o_ref, hn_ref, *, final_norm):
    j = pl.program_id(1)

    @pl.when(j == 0)
    def _():
        x = x_ref[...]
        hn_ref[...] = _rms_norm(x, g_ref[...]).astype(BF16)
        o_ref[...] = x

    a = jnp.maximum(_dot(hn_ref[...], wu_ref[...]), 0.0)
    o_ref[...] += _dot((a * a).astype(BF16), wd_ref[...])

    if final_norm:
        @pl.when(j == pl.num_programs(1) - 1)
        def _():
            o_ref[...] = _rms_norm(o_ref[...], gf_ref[...])


def _mlp(x, gain, w_up, w_down, gain_final, *, tm, fc, final_norm):
    t, d = x.shape
    f = w_up.shape[1]
    est = 2 * 2 * tm * d * 4 + tm * d * 2 + 2 * 2 * d * fc * 2 + 3 * tm * fc * 4
    return pl.pallas_call(
        functools.partial(_mlp_kernel, final_norm=final_norm),
        grid=(t // tm, f // fc),
        in_specs=[
            pl.BlockSpec((tm, d), lambda i, j: (i, 0)),
            pl.BlockSpec((1, d), lambda i, j: (0, 0)),
            pl.BlockSpec((d, fc), lambda i, j: (0, j)),
            pl.BlockSpec((fc, d), lambda i, j: (j, 0)),
            pl.BlockSpec((1, d), lambda i, j: (0, 0)),
        ],
        out_specs=pl.BlockSpec((tm, d), lambda i, j: (i, 0)),
        out_shape=jax.ShapeDtypeStruct((t, d), F32),
        scratch_shapes=[pltpu.VMEM((tm, d), BF16)],
        compiler_params=_params(est),
        name="mlp_final" if final_norm else "mlp",
    )(x, gain, w_up, w_down, gain_final)


def _qkv_kernel(x_ref, g_ref, w_ref, cos_ref, sin_ref, o_ref, hn_ref):
    @pl.when(pl.program_id(1) == 0)
    def _():
        hn_ref[...] = _rms_norm(x_ref[...], g_ref[...]).astype(BF16)

    y = _dot(hn_ref[...], w_ref[...])
    lane = lax.broadcasted_iota(jnp.int32, (1, V7X_LANES), 1)
    first_half = (lane % HEAD_DIM) < HALF_DIM
    cos, sin = cos_ref[...], sin_ref[...]
    for c in range(y.shape[1] // V7X_LANES):
        sl = slice(c * V7X_LANES, (c + 1) * V7X_LANES)
        t = y[:, sl]
        rot = jnp.where(first_half,
                        pltpu.roll(t, V7X_LANES - HALF_DIM, axis=1),
                        pltpu.roll(t, HALF_DIM, axis=1))
        o_ref[:, sl] = (t * cos + rot * sin).astype(BF16)


def _qkv(x, gain, w_ext, cos_t, sin_t, *, tm, nc, n_q_chunks, tiles_per_seq):
    t, d = x.shape
    nw = w_ext.shape[1]
    n_kinds = cos_t.shape[1] // V7X_LANES
    kind = lambda j: jnp.clip(j - (n_q_chunks - 1), 0, n_kinds - 1)
    est = 2 * tm * d * 4 + tm * d * 2 + 2 * d * nc * 2 + 2 * 2 * tm * 128 * 4 + 2 * tm * nc * 2 + 4 * tm * nc * 4
    return pl.pallas_call(
        _qkv_kernel,
        grid=(t // tm, nw // nc),
        in_specs=[
            pl.BlockSpec((tm, d), lambda i, j: (i, 0)),
            pl.BlockSpec((1, d), lambda i, j: (0, 0)),
            pl.BlockSpec((d, nc), lambda i, j: (0, j)),
            pl.BlockSpec((tm, V7X_LANES), lambda i, j: (i % tiles_per_seq, kind(j))),
            pl.BlockSpec((tm, V7X_LANES), lambda i, j: (i % tiles_per_seq, kind(j))),
        ],
        out_specs=pl.BlockSpec((tm, nc), lambda i, j: (i, j)),
        out_shape=jax.ShapeDtypeStruct((t, nw), BF16),
        scratch_shapes=[pltpu.VMEM((tm, d), BF16)],
        compiler_params=_params(est),
        name="qkv_rope",
    )(x, gain, w_ext, cos_t, sin_t)


def _attn_kernel(sink_ref, q_ref, kvc_ref, kvp_ref, kvm_ref, o_ref, *, n_kv, pad):
    n = pl.program_id(1)
    blk = ATTN_BLOCK
    kv_prev = jnp.where(n == 0, kvm_ref[...], kvp_ref[...])
    band = jnp.concatenate([kv_prev, kvc_ref[...]], axis=0)

    qi = lax.broadcasted_iota(jnp.int32, (blk, 2 * blk), 0)
    kj = lax.broadcasted_iota(jnp.int32, (blk, 2 * blk), 1)
    allowed = (kj > qi) & (kj <= qi + blk) & (kj + n * blk >= pad)

    low = lax.broadcasted_iota(jnp.int32, (2 * blk, V7X_LANES), 1) < HEAD_DIM
    zero = jnp.zeros((), BF16)
    pairs = 4
    contract_lanes = (((1,), (1,)), ((), ()))

    for h in range(n_kv):
        kd = band[:, h * V7X_LANES:(h + 1) * V7X_LANES]
        vd = band[:, (n_kv + h) * V7X_LANES:(n_kv + h + 1) * V7X_LANES]
        k_sides = (jnp.where(low, kd, zero), jnp.where(low, zero, kd))
        v_sides = (jnp.where(low, vd, zero), jnp.where(low, zero, vd))
        qbase = h * pairs * V7X_LANES
        qst = jnp.concatenate(
            [q_ref[:, qbase + p * V7X_LANES: qbase + (p + 1) * V7X_LANES] for p in range(pairs)],
            axis=0)
        acc = None
        for side in range(2):
            s_all = lax.dot_general(qst, k_sides[side], contract_lanes, preferred_element_type=F32)
            es, rs = [], []
            for p in range(pairs):
                s = jnp.where(allowed, s_all[p * blk:(p + 1) * blk], NEG_INF)
                sink = sink_ref[h * 2 * pairs + 2 * p + side]
                m = jnp.maximum(jnp.max(s, axis=-1, keepdims=True), sink)
                e = jnp.exp(s - m)
                den = jnp.sum(e, axis=-1, keepdims=True) + jnp.exp(sink - m)
                es.append(e.astype(BF16))
                rs.append(1.0 / den)
            e_all = jnp.concatenate(es, axis=0)
            r_all = jnp.concatenate(rs, axis=0)
            part = _dot(e_all, v_sides[side]) * r_all
            acc = part if acc is None else acc + part
        for p in range(pairs):
            o_ref[:, qbase + p * V7X_LANES: qbase + (p + 1) * V7X_LANES] = (
                acc[p * blk:(p + 1) * blk].astype(BF16))


def _attention(sinks, qkv, kv_meta, *, d, batch, n_kv, pad):
    t = qkv.shape[0]
    blk = ATTN_BLOCK
    nb = t // batch // blk
    kvw = kv_meta.shape[1]
    kv_col = d // kvw
    assert kv_col * kvw == d and qkv.shape[1] == d + kvw
    est = 2 * 2 * blk * d * 2 + 3 * 2 * blk * kvw * 2 + 24 * 512 * 256 * 4
    return pl.pallas_call(
        functools.partial(_attn_kernel, n_kv=n_kv, pad=pad),
        grid=(batch, nb),
        in_specs=[
            pl.BlockSpec(memory_space=pltpu.SMEM),
            pl.BlockSpec((blk, d), lambda b, n: (b * nb + n, 0)),
            pl.BlockSpec((blk, kvw), lambda b, n: (b * nb + n, kv_col)),
            pl.BlockSpec((blk, kvw), lambda b, n: (b * nb + jnp.maximum(n - 1, 0), kv_col)),
            pl.BlockSpec((blk, kvw), lambda b, n: (0, 0)),
        ],
        out_specs=pl.BlockSpec((blk, d), lambda b, n: (b * nb + n, 0)),
        out_shape=jax.ShapeDtypeStruct((t, d), BF16),
        compiler_params=_params(est),
        name="swa_attention",
    )(sinks, qkv, qkv, qkv, kv_meta)


def _out_proj_kernel(a_ref, w_ref, h_ref, o_ref):
    o_ref[...] = h_ref[...] + _dot(a_ref[...], w_ref[...])


def _out_proj(a, w, h, *, tm):
    t, d = h.shape
    k = a.shape[1]
    est = 2 * tm * k * 2 + 2 * k * d * 2 + 2 * 2 * tm * d * 4 + tm * d * 4
    return pl.pallas_call(
        _out_proj_kernel,
        grid=(t // tm, 1),
        in_specs=[
            pl.BlockSpec((tm, k), lambda i, j: (i, 0)),
            pl.BlockSpec((k, d), lambda i, j: (0, 0)),
            pl.BlockSpec((tm, d), lambda i, j: (i, 0)),
        ],
        out_specs=pl.BlockSpec((tm, d), lambda i, j: (i, 0)),
        out_shape=jax.ShapeDtypeStruct((t, d), F32),
        compiler_params=_params(est),
        name="attn_out_proj",
    )(a, w, h)


def _rope_tables(n_pos):
    pos = jnp.arange(n_pos, dtype=F32)
    inv = ROPE_THETA ** (-jnp.arange(0, HEAD_DIM, 2, dtype=F32) / HEAD_DIM)
    ang = pos[:, None] * inv[None, :]
    cos, sin = jnp.cos(ang), jnp.sin(ang)
    cos = jnp.concatenate([cos, cos, cos, cos], axis=1)
    sin = jnp.concatenate([-sin, sin, -sin, sin], axis=1)
    scale = HEAD_DIM ** -0.5
    return (jnp.concatenate([cos * scale, cos, jnp.ones_like(cos)], axis=1),
            jnp.concatenate([sin * scale, sin, jnp.zeros_like(sin)], axis=1))


def _duplicate_heads(w, n_heads):
    w = w.reshape(w.shape[0], n_heads, 1, HEAD_DIM)
    return jnp.broadcast_to(w, (w.shape[0], n_heads, 2, HEAD_DIM)).reshape(w.shape[0], n_heads * 2 * HEAD_DIM)


def kernel(x, meta_tokens, norm_mix_0, w_in_conv, conv_w, w_out_conv, norm_mlp_0, w_up_0, w_down_0,
           norm_mix_1, w_qkv, attn_sinks, w_o, norm_mlp_1, w_up_1, w_down_1, norm_final):
    batch, seq, d = x.shape
    n_meta = meta_tokens.shape[0]
    n_kv = (w_qkv.shape[1] - d) // (2 * HEAD_DIM)
    pad = (-(seq + n_meta)) % ATTN_BLOCK
    assert seq % ATTN_BLOCK == 0 and pad + n_meta == ATTN_BLOCK
    assert n_meta % (2 * V7X_SUBLANES) == 0 and d // HEAD_DIM == 8 * n_kv

    tm, cc, nc, fc = TOKEN_TILE, CONV_CHUNK, COL_CHUNK, FF_CHUNK
    assert seq % tm == 0 and d % nc == 0
    tiles_per_seq = seq // tm
    n_q_chunks = d // nc

    row = lambda g: g.reshape(1, d)
    g_mix0, g_mlp0, g_mix1, g_mlp1, g_fin = map(row, (norm_mix_0, norm_mlp_0, norm_mix_1, norm_mlp_1, norm_final))
    w_in, w_out, wu0, wd0, wqkv, wo, wu1, wd1 = (
        w.astype(BF16) for w in (w_in_conv, w_out_conv, w_up_0, w_down_0, w_qkv, w_o, w_up_1, w_down_1))
    kw = n_kv * HEAD_DIM
    w_ext = jnp.concatenate([wqkv[:, :d], _duplicate_heads(wqkv[:, d:d + kw], n_kv),
                             _duplicate_heads(wqkv[:, d + kw:], n_kv)], axis=1)
    assert 2 * kw == nc, "duplicated k heads (and v heads) must each fill exactly one column chunk"
    cos_t, sin_t = _rope_tables(n_meta + seq)

    zeros_tail = jnp.zeros((V7X_SUBLANES, d), F32)
    hm, tail_m = _conv_mixer(meta_tokens, g_mix0, w_in, conv_w, w_out, zeros_tail,
                             tm=n_meta, nc=cc, tiles_per_seq=1)
    hm = _mlp(hm, g_mlp0, wu0, wd0, g_fin, tm=n_meta, fc=fc, final_norm=False)
    qkv_m = _qkv(hm, g_mix1, w_ext, cos_t[:n_meta], sin_t[:n_meta],
                 tm=n_meta, nc=nc, n_q_chunks=n_q_chunks, tiles_per_seq=1)
    kv_meta = jnp.pad(qkv_m[:, d:], ((pad, 0), (0, 0)))

    h = x.reshape(batch * seq, d)
    assert seq % CONV_TILE == 0
    h, _ = _conv_mixer(h, g_mix0, w_in, conv_w, w_out, tail_m[-V7X_SUBLANES:],
                       tm=CONV_TILE, nc=cc, tiles_per_seq=seq // CONV_TILE)
    h = _mlp(h, g_mlp0, wu0, wd0, g_fin, tm=tm, fc=fc, final_norm=False)
    qkv = _qkv(h, g_mix1, w_ext, cos_t[n_meta:], sin_t[n_meta:],
               tm=tm, nc=nc, n_q_chunks=n_q_chunks, tiles_per_seq=tiles_per_seq)
    a = _attention(attn_sinks, qkv, kv_meta, d=d, batch=batch, n_kv=n_kv, pad=pad)
    h = _out_proj(a, wo, h, tm=OUT_PROJ_TILE)
    out = _mlp(h, g_mlp1, wu1, wd1, g_fin, tm=tm, fc=fc, final_norm=True)
    return out.reshape(batch, seq, d)
```

```python
import functools
from typing import Callable, NamedTuple

import jax
import jax.numpy as jnp
from jax import lax
from jax.experimental import pallas as pl
from jax.experimental.pallas import tpu as pltpu

HEAD_DIM = 64
HALF_DIM = HEAD_DIM // 2
ATTN_BLOCK = 128
ROPE_THETA = 10000.0
RMS_EPS = 1e-5
NEG_INF = -1e30
CONV_WIDTH = 3

V7X_LANES = 128
V7X_SUBLANES = 8
V7X_BF16_ROWS = 16
V7X_VMEM_LIMIT_BYTES = 60000 * 1024

TOKEN_TILE = 1024
CONV_TILE = 512
OUT_PROJ_TILE = 512
CONV_CHUNK = 512
COL_CHUNK = 512
FF_CHUNK = 512
QKV_ROW_SPLITS = 8

BF16 = jnp.bfloat16
F32 = jnp.float32


def _dot(a, b):
    return jnp.dot(a, b, preferred_element_type=F32)


def _rms_norm(x, g):
    var = jnp.mean(x * x, axis=-1, keepdims=True)
    return (x * lax.rsqrt(var + RMS_EPS)) * g


def _row_splits(tm, splits):
    return splits if tm % (splits * 128) == 0 else 1


def _params(est_bytes):
    limit = min(V7X_VMEM_LIMIT_BYTES, max(int(est_bytes * 1.25), 16 * 1024 * 1024))
    return pltpu.CompilerParams(
        dimension_semantics=("arbitrary", "arbitrary"), vmem_limit_bytes=limit)


class _SideCast(NamedTuple):
    w: jax.Array
    out_cols: int
    fn: Callable


def _plain_cast(w):
    return _SideCast(w, w.shape[1], lambda blk: blk.astype(BF16))


def _side_cast_specs(sides, nt, nj):
    n_steps = nt * nj
    in_specs, out_specs, out_shapes, vmem = [], [], [], 0
    for s in sides:
        rows, cols = s.w.shape
        n_blocks = min(n_steps, rows // V7X_BF16_ROWS)
        rb, rep = rows // n_blocks, n_steps // n_blocks
        assert rb * n_blocks == rows and rep * n_blocks == n_steps
        index = lambda i, j, rep=rep: ((i * nj + j) // rep, 0)
        in_specs.append(pl.BlockSpec((rb, cols), index))
        out_specs.append(pl.BlockSpec((rb, s.out_cols), index))
        out_shapes.append(jax.ShapeDtypeStruct((rows, s.out_cols), BF16))
        vmem += 2 * rb * (cols * 4 + s.out_cols * 2)
    return in_specs, out_specs, out_shapes, vmem


def _run_side_casts(fns, in_refs, out_refs):
    for fn, w_ref, o_ref in zip(fns, in_refs, out_refs):
        o_ref[...] = fn(w_ref[...])


def _conv_mixer_kernel(*refs, tiles_per_seq, side_fns):
    n = len(side_fns)
    x_ref, g_ref, wb_ref, wc_ref, wu_ref, cw_ref, wo_ref, tail_in_ref = refs[:8]
    o_ref, tail_out_ref = refs[8 + n:10 + n]
    hn_ref, carry_ref = refs[10 + 2 * n:]
    i = pl.program_id(0)
    j = pl.program_id(1)

    @pl.when(j == 0)
    def _():
        x = x_ref[...]
        hn_ref[...] = _rms_norm(x, g_ref[...]).astype(BF16)
        o_ref[...] = x

    hn = hn_ref[...]
    gate_b = _dot(hn, wb_ref[...])
    v = _dot(hn, wc_ref[...]) * _dot(hn, wu_ref[...])
    tm = v.shape[0]

    first_tile_of_seq = (i % tiles_per_seq) == 0
    prev = jnp.where(first_tile_of_seq, tail_in_ref[...], carry_ref[j])
    vcat = jnp.concatenate([prev, v], axis=0)
    v_m1 = pltpu.roll(vcat, 1, axis=0)[V7X_SUBLANES:]
    v_m2 = pltpu.roll(vcat, 2, axis=0)[V7X_SUBLANES:]
    cw = cw_ref[...]
    conv = cw[0:1] * v_m2 + cw[1:2] * v_m1 + cw[2:3] * v
    gated = (gate_b * conv).astype(BF16)
    o_ref[...] += _dot(gated, wo_ref[...])

    tail = v[tm - V7X_SUBLANES:]
    carry_ref[j] = tail
    tail_out_ref[...] = tail
    _run_side_casts(side_fns, refs[8:8 + n], refs[10 + n:10 + 2 * n])


def _conv_mixer(x, gain, w_in, conv_w, w_out, tail_in, *, tm, nc, tiles_per_seq, side_casts=()):
    t, d = x.shape
    nt, nj = t // tm, d // nc
    side_in, side_out, side_shapes, side_vmem = _side_cast_specs(side_casts, nt, nj)
    est = (2 * 2 * tm * d * 4 + tm * d * 2 + 2 * 4 * d * nc * 2 + 8 * tm * nc * 4) + side_vmem
    kern = functools.partial(_conv_mixer_kernel, tiles_per_seq=tiles_per_seq,
                             side_fns=tuple(s.fn for s in side_casts))
    return pl.pallas_call(
        kern,
        grid=(nt, nj),
        in_specs=[
            pl.BlockSpec((tm, d), lambda i, j: (i, 0)),
            pl.BlockSpec((1, d), lambda i, j: (0, 0)),
            pl.BlockSpec((d, nc), lambda i, j: (0, j)),
            pl.BlockSpec((d, nc), lambda i, j: (0, nj + j)),
            pl.BlockSpec((d, nc), lambda i, j: (0, 2 * nj + j)),
            pl.BlockSpec((CONV_WIDTH, nc), lambda i, j: (0, j)),
            pl.BlockSpec((nc, d), lambda i, j: (j, 0)),
            pl.BlockSpec((V7X_SUBLANES, nc), lambda i, j: (0, j)),
        ] + side_in,
        out_specs=[
            pl.BlockSpec((tm, d), lambda i, j: (i, 0)),
            pl.BlockSpec((V7X_SUBLANES, nc), lambda i, j: (i, j)),
        ] + side_out,
        out_shape=[
            jax.ShapeDtypeStruct((t, d), F32),
            jax.ShapeDtypeStruct((nt * V7X_SUBLANES, d), F32),
        ] + side_shapes,
        scratch_shapes=[
            pltpu.VMEM((tm, d), BF16),
            pltpu.VMEM((nj, V7X_SUBLANES, nc), F32),
        ],
        compiler_params=_params(est),
        name="conv_mixer",
    )(x, gain, w_in, w_in, w_in, conv_w, w_out, tail_in, *(s.w for s in side_casts))


def _mlp_kernel(*refs, final_norm, side_fns):
    n = len(side_fns)
    x_ref, g_ref, wu_ref, wd_ref, gf_ref = refs[:5]
    o_ref = refs[5 + n]
    hn_ref = refs[6 + 2 * n]
    j = pl.program_id(1)

    @pl.when(j == 0)
    def _():
        x = x_ref[...]
        hn_ref[...] = _rms_norm(x, g_ref[...]).astype(BF16)
        o_ref[...] = x

    _run_side_casts(side_fns, refs[5:5 + n], refs[6 + n:6 + 2 * n])
    a = jnp.maximum(_dot(hn_ref[...], wu_ref[...]), 0.0)
    o_ref[...] += _dot((a * a).astype(BF16), wd_ref[...])

    if final_norm:
        @pl.when(j == pl.num_programs(1) - 1)
        def _():
            o_ref[...] = _rms_norm(o_ref[...], gf_ref[...])


def _mlp(x, gain, w_up, w_down, gain_final, *, tm, fc, final_norm, side_casts=()):
    t, d = x.shape
    f = w_up.shape[1]
    nt, nj = t // tm, f // fc
    side_in, side_out, side_shapes, side_vmem = _side_cast_specs(side_casts, nt, nj)
    est = 2 * 2 * tm * d * 4 + tm * d * 2 + 2 * 2 * d * fc * 2 + 3 * tm * fc * 4 + side_vmem
    kern = functools.partial(_mlp_kernel, final_norm=final_norm, side_fns=tuple(s.fn for s in side_casts))
    return pl.pallas_call(
        kern,
        grid=(nt, nj),
        in_specs=[
            pl.BlockSpec((tm, d), lambda i, j: (i, 0)),
            pl.BlockSpec((1, d), lambda i, j: (0, 0)),
            pl.BlockSpec((d, fc), lambda i, j: (0, j)),
            pl.BlockSpec((fc, d), lambda i, j: (j, 0)),
            pl.BlockSpec((1, d), lambda i, j: (0, 0)),
        ] + side_in,
        out_specs=[pl.BlockSpec((tm, d), lambda i, j: (i, 0))] + side_out,
        out_shape=[jax.ShapeDtypeStruct((t, d), F32)] + side_shapes,
        scratch_shapes=[pltpu.VMEM((tm, d), BF16)],
        compiler_params=_params(est),
        name="mlp_final" if final_norm else "mlp",
    )(x, gain, w_up, w_down, gain_final, *(s.w for s in side_casts))


def _qkv_kernel(x_ref, g_ref, w_ref, cos_ref, sin_ref, o_ref, hn_ref):
    @pl.when(pl.program_id(1) == 0)
    def _():
        hn_ref[...] = _rms_norm(x_ref[...], g_ref[...]).astype(BF16)

    lane = lax.broadcasted_iota(jnp.int32, (1, V7X_LANES), 1)
    first_half = (lane % HEAD_DIM) < HALF_DIM
    tm = hn_ref.shape[0]
    rm = tm // _row_splits(tm, QKV_ROW_SPLITS)
    for r0 in range(0, tm, rm):
        rows = slice(r0, r0 + rm)
        y = _dot(hn_ref[rows, :], w_ref[...])
        cos, sin = cos_ref[rows, :], sin_ref[rows, :]
        for c in range(y.shape[1] // V7X_LANES):
            sl = slice(c * V7X_LANES, (c + 1) * V7X_LANES)
            t = y[:, sl]
            rot = jnp.where(first_half,
                            pltpu.roll(t, V7X_LANES - HALF_DIM, axis=1),
                            pltpu.roll(t, HALF_DIM, axis=1))
            o_ref[rows, sl] = (t * cos + rot * sin).astype(BF16)


def _qkv(x, gain, w_ext, cos_t, sin_t, *, tm, nc, n_q_chunks, tiles_per_seq):
    t, d = x.shape
    nw = w_ext.shape[1]
    n_kinds = cos_t.shape[1] // V7X_LANES
    kind = lambda j: jnp.clip(j - (n_q_chunks - 1), 0, n_kinds - 1)
    est = 2 * tm * d * 4 + tm * d * 2 + 2 * d * nc * 2 + 2 * 2 * tm * 128 * 4 + 2 * tm * nc * 2 + 4 * tm * nc * 4
    return pl.pallas_call(
        _qkv_kernel,
        grid=(t // tm, nw // nc),
        in_specs=[
            pl.BlockSpec((tm, d), lambda i, j: (i, 0)),
            pl.BlockSpec((1, d), lambda i, j: (0, 0)),
            pl.BlockSpec((d, nc), lambda i, j: (0, j)),
            pl.BlockSpec((tm, V7X_LANES), lambda i, j: (i % tiles_per_seq, kind(j))),
            pl.BlockSpec((tm, V7X_LANES), lambda i, j: (i % tiles_per_seq, kind(j))),
        ],
        out_specs=pl.BlockSpec((tm, nc), lambda i, j: (i, j)),
        out_shape=jax.ShapeDtypeStruct((t, nw), BF16),
        scratch_shapes=[pltpu.VMEM((tm, d), BF16)],
        compiler_params=_params(est),
        name="qkv_rope",
    )(x, gain, w_ext, cos_t, sin_t)


def _attn_kernel(sink_ref, q_ref, kvc_ref, kvp_ref, kvm_ref, o_ref, *, n_kv, pad):
    n = pl.program_id(1)
    blk = ATTN_BLOCK
    kv_prev = jnp.where(n == 0, kvm_ref[...], kvp_ref[...])
    band = jnp.concatenate([kv_prev, kvc_ref[...]], axis=0)

    qi = lax.broadcasted_iota(jnp.int32, (blk, 2 * blk), 0)
    kj = lax.broadcasted_iota(jnp.int32, (blk, 2 * blk), 1)
    allowed = (kj > qi) & (kj <= qi + blk) & (kj + n * blk >= pad)

    low = lax.broadcasted_iota(jnp.int32, (2 * blk, V7X_LANES), 1) < HEAD_DIM
    zero = jnp.zeros((), BF16)
    pairs = 4
    contract_lanes = (((1,), (1,)), ((), ()))

    for h in range(n_kv):
        kd = band[:, h * V7X_LANES:(h + 1) * V7X_LANES]
        vd = band[:, (n_kv + h) * V7X_LANES:(n_kv + h + 1) * V7X_LANES]
        k_sides = (jnp.where(low, kd, zero), jnp.where(low, zero, kd))
        v_sides = (jnp.where(low, vd, zero), jnp.where(low, zero, vd))
        qbase = h * pairs * V7X_LANES
        qst = jnp.concatenate(
            [q_ref[:, qbase + p * V7X_LANES: qbase + (p + 1) * V7X_LANES] for p in range(pairs)],
            axis=0)
        acc = None
        for side in range(2):
            s_all = lax.dot_general(qst, k_sides[side], contract_lanes, preferred_element_type=F32)
            es, rs = [], []
            for p in range(pairs):
                s = jnp.where(allowed, s_all[p * blk:(p + 1) * blk], NEG_INF)
                sink = sink_ref[h * 2 * pairs + 2 * p + side]
                m = jnp.maximum(jnp.max(s, axis=-1, keepdims=True), sink)
                e = jnp.exp(s - m)
                den = jnp.sum(e, axis=-1, keepdims=True) + jnp.exp(sink - m)
                es.append(e.astype(BF16))
                rs.append(1.0 / den)
            e_all = jnp.concatenate(es, axis=0)
            r_all = jnp.concatenate(rs, axis=0)
            part = _dot(e_all, v_sides[side]) * r_all
            acc = part if acc is None else acc + part
        for p in range(pairs):
            o_ref[:, qbase + p * V7X_LANES: qbase + (p + 1) * V7X_LANES] = (
                acc[p * blk:(p + 1) * blk].astype(BF16))


def _attention(sinks, qkv, kv_meta, *, d, batch, n_kv, pad):
    t = qkv.shape[0]
    blk = ATTN_BLOCK
    nb = t // batch // blk
    kvw = kv_meta.shape[1]
    kv_col = d // kvw
    assert kv_col * kvw == d and qkv.shape[1] == d + kvw
    est = 2 * 2 * blk * d * 2 + 3 * 2 * blk * kvw * 2 + 24 * 512 * 256 * 4
    return pl.pallas_call(
        functools.partial(_attn_kernel, n_kv=n_kv, pad=pad),
        grid=(batch, nb),
        in_specs=[
            pl.BlockSpec(memory_space=pltpu.SMEM),
            pl.BlockSpec((blk, d), lambda b, n: (b * nb + n, 0)),
            pl.BlockSpec((blk, kvw), lambda b, n: (b * nb + n, kv_col)),
            pl.BlockSpec((blk, kvw), lambda b, n: (b * nb + jnp.maximum(n - 1, 0), kv_col)),
            pl.BlockSpec((blk, kvw), lambda b, n: (0, 0)),
        ],
        out_specs=pl.BlockSpec((blk, d), lambda b, n: (b * nb + n, 0)),
        out_shape=jax.ShapeDtypeStruct((t, d), BF16),
        compiler_params=_params(est),
        name="swa_attention",
    )(sinks, qkv, qkv, qkv, kv_meta)


def _out_proj_kernel(a_ref, w_ref, h_ref, o_ref):
    o_ref[...] = h_ref[...] + _dot(a_ref[...], w_ref[...])


def _out_proj(a, w, h, *, tm):
    t, d = h.shape
    k = a.shape[1]
    est = 2 * tm * k * 2 + 2 * k * d * 2 + 2 * 2 * tm * d * 4 + tm * d * 4
    return pl.pallas_call(
        _out_proj_kernel,
        grid=(t // tm, 1),
        in_specs=[
            pl.BlockSpec((tm, k), lambda i, j: (i, 0)),
            pl.BlockSpec((k, d), lambda i, j: (0, 0)),
            pl.BlockSpec((tm, d), lambda i, j: (i, 0)),
        ],
        out_specs=pl.BlockSpec((tm, d), lambda i, j: (i, 0)),
        out_shape=jax.ShapeDtypeStruct((t, d), F32),
        compiler_params=_params(est),
        name="attn_out_proj",
    )(a, w, h)


def _rope_tables(n_pos):
    pos = jnp.arange(n_pos, dtype=F32)
    inv = ROPE_THETA ** (-jnp.arange(0, HEAD_DIM, 2, dtype=F32) / HEAD_DIM)
    ang = pos[:, None] * inv[None, :]
    cos, sin = jnp.cos(ang), jnp.sin(ang)
    cos = jnp.concatenate([cos, cos, cos, cos], axis=1)
    sin = jnp.concatenate([-sin, sin, -sin, sin], axis=1)
    scale = HEAD_DIM ** -0.5
    return (jnp.concatenate([cos * scale, cos, jnp.ones_like(cos)], axis=1),
            jnp.concatenate([sin * scale, sin, jnp.zeros_like(sin)], axis=1))


def _extend_qkv_block(blk, *, d):
    low = lax.broadcasted_iota(jnp.int32, (1, V7X_LANES), 1) < HEAD_DIM
    pieces = [blk[:, :d].astype(BF16)]
    for c0 in range(d, blk.shape[1], V7X_LANES):
        t = blk[:, c0:c0 + V7X_LANES]
        swapped = pltpu.roll(t, HEAD_DIM, axis=1)
        pieces += [jnp.where(low, t, swapped).astype(BF16), jnp.where(low, swapped, t).astype(BF16)]
    return jnp.concatenate(pieces, axis=1)


def kernel(x, meta_tokens, norm_mix_0, w_in_conv, conv_w, w_out_conv, norm_mlp_0, w_up_0, w_down_0,
           norm_mix_1, w_qkv, attn_sinks, w_o, norm_mlp_1, w_up_1, w_down_1, norm_final):
    batch, seq, d = x.shape
    n_meta = meta_tokens.shape[0]
    n_kv = (w_qkv.shape[1] - d) // (2 * HEAD_DIM)
    pad = (-(seq + n_meta)) % ATTN_BLOCK
    assert seq % ATTN_BLOCK == 0 and pad + n_meta == ATTN_BLOCK
    assert n_meta % (2 * V7X_SUBLANES) == 0 and d // HEAD_DIM == 8 * n_kv

    tm, cc, nc, fc = TOKEN_TILE, CONV_CHUNK, COL_CHUNK, FF_CHUNK
    assert seq % tm == 0 and d % nc == 0
    tiles_per_seq = seq // tm
    n_q_chunks = d // nc

    row = lambda g: g.reshape(1, d)
    g_mix0, g_mlp0, g_mix1, g_mlp1, g_fin = map(row, (norm_mix_0, norm_mlp_0, norm_mix_1, norm_mlp_1, norm_final))
    kw = n_kv * HEAD_DIM
    assert 2 * kw == nc, "duplicated k heads (and v heads) must each fill exactly one column chunk"
    cos_t, sin_t = _rope_tables(n_meta + seq)
    w_in, w_out = w_in_conv.astype(BF16), w_out_conv.astype(BF16)

    zeros_tail = jnp.zeros((V7X_SUBLANES, d), F32)
    hm, tail_m = _conv_mixer(meta_tokens, g_mix0, w_in, conv_w, w_out, zeros_tail,
                             tm=n_meta, nc=cc, tiles_per_seq=1)
    h = x.reshape(batch * seq, d)
    assert seq % CONV_TILE == 0
    h, _, wu0, wd0 = _conv_mixer(h, g_mix0, w_in, conv_w, w_out, tail_m[-V7X_SUBLANES:],
                                 tm=CONV_TILE, nc=cc, tiles_per_seq=seq // CONV_TILE,
                                 side_casts=(_plain_cast(w_up_0), _plain_cast(w_down_0)))

    (hm,) = _mlp(hm, g_mlp0, wu0, wd0, g_fin, tm=n_meta, fc=fc, final_norm=False)
    qkv_cast = _SideCast(w_qkv, d + 4 * kw, functools.partial(_extend_qkv_block, d=d))
    h, w_ext, wo, wu1, wd1 = _mlp(h, g_mlp0, wu0, wd0, g_fin, tm=tm, fc=fc, final_norm=False,
                                  side_casts=(qkv_cast, _plain_cast(w_o), _plain_cast(w_up_1),
                                              _plain_cast(w_down_1)))

    qkv_m = _qkv(hm, g_mix1, w_ext, cos_t[:n_meta], sin_t[:n_meta],
                 tm=n_meta, nc=nc, n_q_chunks=n_q_chunks, tiles_per_seq=1)
    kv_meta = jnp.pad(qkv_m[:, d:], ((pad, 0), (0, 0)))
    qkv = _qkv(h, g_mix1, w_ext, cos_t[n_meta:], sin_t[n_meta:],
               tm=tm, nc=nc, n_q_chunks=n_q_chunks, tiles_per_seq=tiles_per_seq)
    a = _attention(attn_sinks, qkv, kv_meta, d=d, batch=batch, n_kv=n_kv, pad=pad)
    h = _out_proj(a, wo, h, tm=OUT_PROJ_TILE)
    (out,) = _mlp(h, g_mlp1, wu1, wd1, g_fin, tm=tm, fc=fc, final_norm=True)
    return out.reshape(batch, seq, d)
```

```python
import functools
from typing import Callable, NamedTuple

import jax
import jax.numpy as jnp
from jax import lax
from jax.experimental import pallas as pl
from jax.experimental.pallas import tpu as pltpu

HEAD_DIM = 64
HALF_DIM = HEAD_DIM // 2
ATTN_BLOCK = 128
ROPE_THETA = 10000.0
RMS_EPS = 1e-5
NEG_INF = -1e30
CONV_WIDTH = 3

V7X_LANES = 128
V7X_SUBLANES = 8
V7X_BF16_ROWS = 16
V7X_VMEM_LIMIT_BYTES = 60000 * 1024

TOKEN_TILE = 1024
CONV_TILE = 512
OUT_PROJ_TILE = 512
CONV_CHUNK = 512
COL_CHUNK = 512
FF_CHUNK = 512
QKV_ROW_SPLITS = 8

BF16 = jnp.bfloat16
F32 = jnp.float32


def _dot(a, b):
    return jnp.dot(a, b, preferred_element_type=F32)


def _rms_norm(x, g):
    var = jnp.mean(x * x, axis=-1, keepdims=True)
    return (x * lax.rsqrt(var + RMS_EPS)) * g


def _row_splits(tm, splits):
    return splits if tm % (splits * 128) == 0 else 1


def _params(est_bytes):
    limit = min(V7X_VMEM_LIMIT_BYTES, max(int(est_bytes * 1.25), 16 * 1024 * 1024))
    return pltpu.CompilerParams(
        dimension_semantics=("arbitrary", "arbitrary"), vmem_limit_bytes=limit)


class _SideCast(NamedTuple):
    w: jax.Array
    out_cols: int
    fn: Callable


def _plain_cast(w):
    return _SideCast(w, w.shape[1], lambda blk: blk.astype(BF16))


def _side_cast_specs(sides, nt, nj):
    n_steps = nt * nj
    in_specs, out_specs, out_shapes, vmem = [], [], [], 0
    for s in sides:
        rows, cols = s.w.shape
        n_blocks = min(n_steps, rows // V7X_BF16_ROWS)
        rb, rep = rows // n_blocks, n_steps // n_blocks
        assert rb * n_blocks == rows and rep * n_blocks == n_steps
        index = lambda i, j, rep=rep: ((i * nj + j) // rep, 0)
        in_specs.append(pl.BlockSpec((rb, cols), index))
        out_specs.append(pl.BlockSpec((rb, s.out_cols), index))
        out_shapes.append(jax.ShapeDtypeStruct((rows, s.out_cols), BF16))
        vmem += 2 * rb * (cols * 4 + s.out_cols * 2)
    return in_specs, out_specs, out_shapes, vmem


def _run_side_casts(fns, in_refs, out_refs):
    for fn, w_ref, o_ref in zip(fns, in_refs, out_refs):
        o_ref[...] = fn(w_ref[...])


def _conv_mixer_kernel(*refs, tiles_per_seq, side_fns):
    n = len(side_fns)
    x_ref, g_ref, wb_ref, wc_ref, wu_ref, cw_ref, wo_ref, tail_in_ref = refs[:8]
    o_ref, tail_out_ref = refs[8 + n:10 + n]
    hn_ref, carry_ref = refs[10 + 2 * n:]
    i = pl.program_id(0)
    j = pl.program_id(1)

    @pl.when(j == 0)
    def _():
        x = x_ref[...]
        hn_ref[...] = _rms_norm(x, g_ref[...]).astype(BF16)
        o_ref[...] = x

    hn = hn_ref[...]
    gate_b = _dot(hn, wb_ref[...])
    v = _dot(hn, wc_ref[...]) * _dot(hn, wu_ref[...])
    tm = v.shape[0]

    first_tile_of_seq = (i % tiles_per_seq) == 0
    prev = jnp.where(first_tile_of_seq, tail_in_ref[...], carry_ref[j])
    vcat = jnp.concatenate([prev, v], axis=0)
    v_m1 = pltpu.roll(vcat, 1, axis=0)[V7X_SUBLANES:]
    v_m2 = pltpu.roll(vcat, 2, axis=0)[V7X_SUBLANES:]
    cw = cw_ref[...]
    conv = cw[0:1] * v_m2 + cw[1:2] * v_m1 + cw[2:3] * v
    gated = (gate_b * conv).astype(BF16)
    o_ref[...] += _dot(gated, wo_ref[...])

    tail = v[tm - V7X_SUBLANES:]
    carry_ref[j] = tail
    tail_out_ref[...] = tail
    _run_side_casts(side_fns, refs[8:8 + n], refs[10 + n:10 + 2 * n])


def _conv_mixer(x, gain, w_in, conv_w, w_out, tail_in, *, tm, nc, tiles_per_seq, side_casts=()):
    t, d = x.shape
    nt, nj = t // tm, d // nc
    side_in, side_out, side_shapes, side_vmem = _side_cast_specs(side_casts, nt, nj)
    est = (2 * 2 * tm * d * 4 + tm * d * 2 + 2 * 4 * d * nc * 2 + 8 * tm * nc * 4) + side_vmem
    kern = functools.partial(_conv_mixer_kernel, tiles_per_seq=tiles_per_seq,
                             side_fns=tuple(s.fn for s in side_casts))
    return pl.pallas_call(
        kern,
        grid=(nt, nj),
        in_specs=[
            pl.BlockSpec((tm, d), lambda i, j: (i, 0)),
            pl.BlockSpec((1, d), lambda i, j: (0, 0)),
            pl.BlockSpec((d, nc), lambda i, j: (0, j)),
            pl.BlockSpec((d, nc), lambda i, j: (0, nj + j)),
            pl.BlockSpec((d, nc), lambda i, j: (0, 2 * nj + j)),
            pl.BlockSpec((CONV_WIDTH, nc), lambda i, j: (0, j)),
            pl.BlockSpec((nc, d), lambda i, j: (j, 0)),
            pl.BlockSpec((V7X_SUBLANES, nc), lambda i, j: (0, j)),
        ] + side_in,
        out_specs=[
            pl.BlockSpec((tm, d), lambda i, j: (i, 0)),
            pl.BlockSpec((V7X_SUBLANES, nc), lambda i, j: (i, j)),
        ] + side_out,
        out_shape=[
            jax.ShapeDtypeStruct((t, d), F32),
            jax.ShapeDtypeStruct((nt * V7X_SUBLANES, d), F32),
        ] + side_shapes,
        scratch_shapes=[
            pltpu.VMEM((tm, d), BF16),
            pltpu.VMEM((nj, V7X_SUBLANES, nc), F32),
        ],
        compiler_params=_params(est),
        name="conv_mixer",
    )(x, gain, w_in, w_in, w_in, conv_w, w_out, tail_in, *(s.w for s in side_casts))


def _mlp_kernel(*refs, final_norm, side_fns):
    n = len(side_fns)
    x_ref, g_ref, wu_ref, wd_ref, gf_ref = refs[:5]
    o_ref = refs[5 + n]
    hn_ref = refs[6 + 2 * n]
    j = pl.program_id(1)

    @pl.when(j == 0)
    def _():
        x = x_ref[...]
        hn_ref[...] = _rms_norm(x, g_ref[...]).astype(BF16)
        o_ref[...] = x

    _run_side_casts(side_fns, refs[5:5 + n], refs[6 + n:6 + 2 * n])
    a = jnp.maximum(_dot(hn_ref[...], wu_ref[...]), 0.0)
    o_ref[...] += _dot((a * a).astype(BF16), wd_ref[...])

    if final_norm:
        @pl.when(j == pl.num_programs(1) - 1)
        def _():
            o_ref[...] = _rms_norm(o_ref[...], gf_ref[...])


def _mlp(x, gain, w_up, w_down, gain_final, *, tm, fc, final_norm, side_casts=()):
    t, d = x.shape
    f = w_up.shape[1]
    nt, nj = t // tm, f // fc
    side_in, side_out, side_shapes, side_vmem = _side_cast_specs(side_casts, nt, nj)
    est = 2 * 2 * tm * d * 4 + tm * d * 2 + 2 * 2 * d * fc * 2 + 3 * tm * fc * 4 + side_vmem
    kern = functools.partial(_mlp_kernel, final_norm=final_norm, side_fns=tuple(s.fn for s in side_casts))
    return pl.pallas_call(
        kern,
        grid=(nt, nj),
        in_specs=[
            pl.BlockSpec((tm, d), lambda i, j: (i, 0)),
            pl.BlockSpec((1, d), lambda i, j: (0, 0)),
            pl.BlockSpec((d, fc), lambda i, j: (0, j)),
            pl.BlockSpec((fc, d), lambda i, j: (j, 0)),
            pl.BlockSpec((1, d), lambda i, j: (0, 0)),
        ] + side_in,
        out_specs=[pl.BlockSpec((tm, d), lambda i, j: (i, 0))] + side_out,
        out_shape=[jax.ShapeDtypeStruct((t, d), F32)] + side_shapes,
        scratch_shapes=[pltpu.VMEM((tm, d), BF16)],
        compiler_params=_params(est),
        name="mlp_final" if final_norm else "mlp",
    )(x, gain, w_up, w_down, gain_final, *(s.w for s in side_casts))


def _qkv_kernel(x_ref, g_ref, w_ref, cos_ref, sin_ref, o_ref, hn_ref, *, v_chunk):
    @pl.when(pl.program_id(1) == 0)
    def _():
        hn_ref[...] = _rms_norm(x_ref[...], g_ref[...]).astype(BF16)

    is_v = pl.program_id(1) == v_chunk
    lane = lax.broadcasted_iota(jnp.int32, (1, V7X_LANES), 1)
    first_half = (lane % HEAD_DIM) < HALF_DIM
    tm = hn_ref.shape[0]
    rm = tm // _row_splits(tm, QKV_ROW_SPLITS)
    for r0 in range(0, tm, rm):
        rows = slice(r0, r0 + rm)
        y = _dot(hn_ref[rows, :], w_ref[...])
        cos = jnp.where(is_v, 1.0, cos_ref[rows, :])
        sin = jnp.where(is_v, 0.0, sin_ref[rows, :])
        for c in range(y.shape[1] // V7X_LANES):
            sl = slice(c * V7X_LANES, (c + 1) * V7X_LANES)
            t = y[:, sl]
            rot = jnp.where(first_half,
                            pltpu.roll(t, V7X_LANES - HALF_DIM, axis=1),
                            pltpu.roll(t, HALF_DIM, axis=1))
            o_ref[rows, sl] = (t * cos + rot * sin).astype(BF16)


def _qkv(x, gain, w_ext, cos_t, sin_t, *, tm, nc, tiles_per_seq):
    t, d = x.shape
    nw = w_ext.shape[1]
    est = 2 * tm * d * 4 + tm * d * 2 + 2 * d * nc * 2 + 2 * 2 * tm * 128 * 4 + 2 * tm * nc * 2 + 4 * tm * nc * 4
    return pl.pallas_call(
        functools.partial(_qkv_kernel, v_chunk=nw // nc - 1),
        grid=(t // tm, nw // nc),
        in_specs=[
            pl.BlockSpec((tm, d), lambda i, j: (i, 0)),
            pl.BlockSpec((1, d), lambda i, j: (0, 0)),
            pl.BlockSpec((d, nc), lambda i, j: (0, j)),
            pl.BlockSpec((tm, V7X_LANES), lambda i, j: (i % tiles_per_seq, 0)),
            pl.BlockSpec((tm, V7X_LANES), lambda i, j: (i % tiles_per_seq, 0)),
        ],
        out_specs=pl.BlockSpec((tm, nc), lambda i, j: (i, j)),
        out_shape=jax.ShapeDtypeStruct((t, nw), BF16),
        scratch_shapes=[pltpu.VMEM((tm, d), BF16)],
        compiler_params=_params(est),
        name="qkv_rope",
    )(x, gain, w_ext, cos_t, sin_t)


def _attn_kernel(sink_ref, q_ref, kvc_ref, kvp_ref, kvm_ref, o_ref, *, n_kv, pad):
    n = pl.program_id(1)
    blk = ATTN_BLOCK
    kv_prev = jnp.where(n == 0, kvm_ref[...], kvp_ref[...])
    band = jnp.concatenate([kv_prev, kvc_ref[...]], axis=0)

    qi = lax.broadcasted_iota(jnp.int32, (blk, 2 * blk), 0)
    kj = lax.broadcasted_iota(jnp.int32, (blk, 2 * blk), 1)
    allowed = (kj > qi) & (kj <= qi + blk) & (kj + n * blk >= pad)

    sink_slot = lax.broadcasted_iota(jnp.int32, (1, 2 * blk), 1) == 0
    low = lax.broadcasted_iota(jnp.int32, (2 * blk, V7X_LANES), 1) < HEAD_DIM
    not_sink_row = lax.broadcasted_iota(jnp.int32, (V7X_BF16_ROWS, V7X_LANES), 0) > 0
    zero = jnp.zeros((), BF16)
    low_row = lax.broadcasted_iota(jnp.int32, (1, V7X_LANES), 1) < HEAD_DIM
    ones_low = jnp.broadcast_to(jnp.where(low_row, 1.0, 0.0).astype(BF16), (2 * blk, V7X_LANES))
    ones_high = jnp.broadcast_to(jnp.where(low_row, 0.0, 1.0).astype(BF16), (2 * blk, V7X_LANES))
    pairs = 4
    contract_lanes = (((1,), (1,)), ((), ()))

    for h in range(n_kv):
        kd = band[:, h * V7X_LANES:(h + 1) * V7X_LANES]
        vd = band[:, (n_kv + h) * V7X_LANES:(n_kv + h + 1) * V7X_LANES]
        sink_tile = jnp.where(not_sink_row, vd[:V7X_BF16_ROWS].astype(F32), 0.0).astype(BF16)
        vd = jnp.concatenate([sink_tile, vd[V7X_BF16_ROWS:]], axis=0)
        k_sides = (jnp.where(low, kd, zero), jnp.where(low, zero, kd))
        values = jnp.concatenate([
            jnp.concatenate([jnp.where(low, vd, zero), ones_low], axis=1),
            jnp.concatenate([jnp.where(low, zero, vd), ones_high], axis=1)], axis=0)
        qbase = h * pairs * V7X_LANES
        qst = jnp.concatenate(
            [q_ref[:, qbase + p * V7X_LANES: qbase + (p + 1) * V7X_LANES] for p in range(pairs)],
            axis=0)
        e_sides = []
        for side in range(2):
            s_all = lax.dot_general(qst, k_sides[side], contract_lanes, preferred_element_type=F32)
            es = []
            for p in range(pairs):
                fill = jnp.where(sink_slot, sink_ref[h * 2 * pairs + 2 * p + side], NEG_INF)
                s = jnp.where(allowed, s_all[p * blk:(p + 1) * blk], fill)
                e = jnp.exp(s - jnp.max(s, axis=-1, keepdims=True))
                es.append(e.astype(BF16))
            e_sides.append(jnp.concatenate(es, axis=0))
        o_den = _dot(jnp.concatenate(e_sides, axis=1), values)
        out = o_den[:, :V7X_LANES] * (1.0 / o_den[:, V7X_LANES:])
        for p in range(pairs):
            o_ref[:, qbase + p * V7X_LANES: qbase + (p + 1) * V7X_LANES] = (
                out[p * blk:(p + 1) * blk].astype(BF16))


def _attention(sinks, qkv, kv_meta, *, d, batch, n_kv, pad):
    t = qkv.shape[0]
    blk = ATTN_BLOCK
    nb = t // batch // blk
    kvw = kv_meta.shape[1]
    kv_col = d // kvw
    assert kv_col * kvw == d and qkv.shape[1] == d + kvw
    est = 2 * 2 * blk * d * 2 + 3 * 2 * blk * kvw * 2 + 24 * 512 * 256 * 4
    return pl.pallas_call(
        functools.partial(_attn_kernel, n_kv=n_kv, pad=pad),
        grid=(batch, nb),
        in_specs=[
            pl.BlockSpec(memory_space=pltpu.SMEM),
            pl.BlockSpec((blk, d), lambda b, n: (b * nb + n, 0)),
            pl.BlockSpec((blk, kvw), lambda b, n: (b * nb + n, kv_col)),
            pl.BlockSpec((blk, kvw), lambda b, n: (b * nb + jnp.maximum(n - 1, 0), kv_col)),
            pl.BlockSpec((blk, kvw), lambda b, n: (0, 0)),
        ],
        out_specs=pl.BlockSpec((blk, d), lambda b, n: (b * nb + n, 0)),
        out_shape=jax.ShapeDtypeStruct((t, d), BF16),
        compiler_params=_params(est),
        name="swa_attention",
    )(sinks, qkv, qkv, qkv, kv_meta)


def _out_proj_kernel(a_ref, w_ref, h_ref, o_ref):
    o_ref[...] = h_ref[...] + _dot(a_ref[...], w_ref[...])


def _out_proj(a, w, h, *, tm):
    t, d = h.shape
    k = a.shape[1]
    est = 2 * tm * k * 2 + 2 * k * d * 2 + 2 * 2 * tm * d * 4 + tm * d * 4
    return pl.pallas_call(
        _out_proj_kernel,
        grid=(t // tm, 1),
        in_specs=[
            pl.BlockSpec((tm, k), lambda i, j: (i, 0)),
            pl.BlockSpec((k, d), lambda i, j: (0, 0)),
            pl.BlockSpec((tm, d), lambda i, j: (i, 0)),
        ],
        out_specs=pl.BlockSpec((tm, d), lambda i, j: (i, 0)),
        out_shape=jax.ShapeDtypeStruct((t, d), F32),
        compiler_params=_params(est),
        name="attn_out_proj",
    )(a, w, h)


def _rope_tables(first_pos, n_pos):
    pos = jnp.arange(first_pos, first_pos + n_pos, dtype=F32)
    inv = ROPE_THETA ** (-jnp.arange(0, HEAD_DIM, 2, dtype=F32) / HEAD_DIM)
    ang = pos[:, None] * inv[None, :]
    cos, sin = jnp.cos(ang), jnp.sin(ang)
    return (jnp.concatenate([cos, cos, cos, cos], axis=1),
            jnp.concatenate([-sin, sin, -sin, sin], axis=1))


def _extend_qkv_block(blk, *, d):
    low = lax.broadcasted_iota(jnp.int32, (1, V7X_LANES), 1) < HEAD_DIM
    pieces = [(blk[:, :d] * HEAD_DIM ** -0.5).astype(BF16)]
    for c0 in range(d, blk.shape[1], V7X_LANES):
        t = blk[:, c0:c0 + V7X_LANES]
        swapped = pltpu.roll(t, HEAD_DIM, axis=1)
        pieces += [jnp.where(low, t, swapped).astype(BF16), jnp.where(low, swapped, t).astype(BF16)]
    return jnp.concatenate(pieces, axis=1)


def kernel(x, meta_tokens, norm_mix_0, w_in_conv, conv_w, w_out_conv, norm_mlp_0, w_up_0, w_down_0,
           norm_mix_1, w_qkv, attn_sinks, w_o, norm_mlp_1, w_up_1, w_down_1, norm_final):
    batch, seq, d = x.shape
    n_meta = meta_tokens.shape[0]
    n_kv = (w_qkv.shape[1] - d) // (2 * HEAD_DIM)
    pad = (-(seq + n_meta)) % ATTN_BLOCK
    assert seq % ATTN_BLOCK == 0 and pad + n_meta == ATTN_BLOCK
    assert n_meta % (2 * V7X_SUBLANES) == 0 and d // HEAD_DIM == 8 * n_kv

    tm, cc, nc, fc = TOKEN_TILE, CONV_CHUNK, COL_CHUNK, FF_CHUNK
    assert seq % tm == 0 and d % nc == 0
    tiles_per_seq = seq // tm
    n_q_chunks = d // nc

    row = lambda g: g.reshape(1, d)
    g_mix0, g_mlp0, g_mix1, g_mlp1, g_fin = map(row, (norm_mix_0, norm_mlp_0, norm_mix_1, norm_mlp_1, norm_final))
    kw = n_kv * HEAD_DIM
    assert 2 * kw == nc, "duplicated k heads (and v heads) must each fill exactly one column chunk"
    cos_m, sin_m = _rope_tables(0, n_meta)
    cos_t, sin_t = _rope_tables(n_meta, seq)
    w_in, w_out = w_in_conv.astype(BF16), w_out_conv.astype(BF16)

    zeros_tail = jnp.zeros((V7X_SUBLANES, d), F32)
    hm, tail_m = _conv_mixer(meta_tokens, g_mix0, w_in, conv_w, w_out, zeros_tail,
                             tm=n_meta, nc=cc, tiles_per_seq=1)
    h = x.reshape(batch * seq, d)
    assert seq % CONV_TILE == 0
    h, _, wu0, wd0 = _conv_mixer(h, g_mix0, w_in, conv_w, w_out, tail_m[-V7X_SUBLANES:],
                                 tm=CONV_TILE, nc=cc, tiles_per_seq=seq // CONV_TILE,
                                 side_casts=(_plain_cast(w_up_0), _plain_cast(w_down_0)))

    (hm,) = _mlp(hm, g_mlp0, wu0, wd0, g_fin, tm=n_meta, fc=fc, final_norm=False)
    qkv_cast = _SideCast(w_qkv, d + 4 * kw, functools.partial(_extend_qkv_block, d=d))
    h, w_ext, wo, wu1, wd1 = _mlp(h, g_mlp0, wu0, wd0, g_fin, tm=tm, fc=fc, final_norm=False,
                                  side_casts=(qkv_cast, _plain_cast(w_o), _plain_cast(w_up_1),
                                              _plain_cast(w_down_1)))

    qkv_m = _qkv(hm, g_mix1, w_ext, cos_m, sin_m, tm=n_meta, nc=nc, tiles_per_seq=1)
    kv_meta = jnp.pad(qkv_m[:, d:], ((pad, 0), (0, 0)))
    qkv = _qkv(h, g_mix1, w_ext, cos_t, sin_t, tm=tm, nc=nc, tiles_per_seq=tiles_per_seq)
    a = _attention(attn_sinks, qkv, kv_meta, d=d, batch=batch, n_kv=n_kv, pad=pad)
    h = _out_proj(a, wo, h, tm=OUT_PROJ_TILE)
    (out,) = _mlp(h, g_mlp1, wu1, wd1, g_fin, tm=tm, fc=fc, final_norm=True)
    return out.reshape(batch, seq, d)
```

```python
import functools
from typing import Callable, NamedTuple

import jax
import jax.numpy as jnp
from jax import lax
from jax.experimental import pallas as pl
from jax.experimental.pallas import tpu as pltpu

HEAD_DIM = 64
HALF_DIM = HEAD_DIM // 2
ATTN_BLOCK = 128
ROPE_THETA = 10000.0
RMS_EPS = 1e-5
NEG_INF = -1e30
CONV_WIDTH = 3

V7X_LANES = 128
V7X_SUBLANES = 8
V7X_BF16_ROWS = 16
V7X_VMEM_LIMIT_BYTES = 60000 * 1024

TOKEN_TILE = 1024
CONV_TILE = 512
OUT_PROJ_TILE = 512
CONV_CHUNK = 512
COL_CHUNK = 512
FF_CHUNK = 1024
QKV_ROW_SPLITS = 8
CONV_NORM_ROW_SPLITS = 2
NORM_ROW_SPLITS = 8

BF16 = jnp.bfloat16
F32 = jnp.float32


def _dot(a, b):
    return jnp.dot(a, b, preferred_element_type=F32)


def _rms_norm(x, g):
    var = jnp.mean(x * x, axis=-1, keepdims=True)
    return (x * lax.rsqrt(var + RMS_EPS)) * g


def _row_splits(tm, splits):
    return splits if tm % (splits * 128) == 0 else 1


def _params(est_bytes):
    limit = min(V7X_VMEM_LIMIT_BYTES, max(int(est_bytes * 1.25), 16 * 1024 * 1024))
    return pltpu.CompilerParams(
        dimension_semantics=("arbitrary", "arbitrary"), vmem_limit_bytes=limit)


class _SideCast(NamedTuple):
    w: jax.Array
    out_cols: int
    fn: Callable


def _plain_cast(w):
    return _SideCast(w, w.shape[1], lambda blk: blk.astype(BF16))


def _side_cast_specs(sides, nt, nj):
    n_steps = nt * nj
    in_specs, out_specs, out_shapes, vmem = [], [], [], 0
    for s in sides:
        rows, cols = s.w.shape
        n_blocks = 1
        while 2 * n_blocks <= min(n_steps, rows // V7X_BF16_ROWS) and rows % (2 * n_blocks) == 0:
            n_blocks *= 2
        rb = rows // n_blocks
        assert rb % V7X_BF16_ROWS == 0
        index = lambda i, j, nb=n_blocks: (((i * nj + j) * nb) // n_steps, 0)
        in_specs.append(pl.BlockSpec((rb, cols), index))
        out_specs.append(pl.BlockSpec((rb, s.out_cols), index))
        out_shapes.append(jax.ShapeDtypeStruct((rows, s.out_cols), BF16))
        vmem += 2 * rb * (cols * 4 + s.out_cols * 2)
    return in_specs, out_specs, out_shapes, vmem


def _run_side_casts(fns, in_refs, out_refs):
    for fn, w_ref, o_ref in zip(fns, in_refs, out_refs):
        o_ref[...] = fn(w_ref[...])


def _conv_mixer_kernel(*refs, tiles_per_seq, side_fns):
    n = len(side_fns)
    x_ref, g_ref, wb_ref, wc_ref, wu_ref, cw_ref, wo_ref, tail_in_ref = refs[:8]
    o_ref, tail_out_ref = refs[8 + n:10 + n]
    hn_ref, carry_ref = refs[10 + 2 * n:]
    i = pl.program_id(0)
    j = pl.program_id(1)
    tm = hn_ref.shape[0]
    cw = cw_ref[...]
    first_tile_of_seq = (i % tiles_per_seq) == 0
    tile_prev = jnp.where(first_tile_of_seq, tail_in_ref[...], carry_ref[j])

    def mix(hn, prev):
        gate_b = _dot(hn, wb_ref[...])
        v = _dot(hn, wc_ref[...]) * _dot(hn, wu_ref[...])
        vcat = jnp.concatenate([prev, v], axis=0)
        v_m1 = pltpu.roll(vcat, 1, axis=0)[V7X_SUBLANES:]
        v_m2 = pltpu.roll(vcat, 2, axis=0)[V7X_SUBLANES:]
        conv = cw[0:1] * v_m2 + cw[1:2] * v_m1 + cw[2:3] * v
        gated = (gate_b * conv).astype(BF16)
        return _dot(gated, wo_ref[...]), v[v.shape[0] - V7X_SUBLANES:]

    def finish(tail):
        carry_ref[j] = tail
        tail_out_ref[...] = tail
        _run_side_casts(side_fns, refs[8:8 + n], refs[10 + n:10 + 2 * n])

    @pl.when(j == 0)
    def _():
        rm = tm // _row_splits(tm, CONV_NORM_ROW_SPLITS)
        prev = tile_prev
        for r0 in range(0, tm, rm):
            rows = slice(r0, r0 + rm)
            x = x_ref[rows, :]
            hn = _rms_norm(x, g_ref[...]).astype(BF16)
            hn_ref[rows, :] = hn
            y, prev = mix(hn, prev)
            o_ref[rows, :] = x + y
        finish(prev)

    @pl.when(j > 0)
    def _():
        y, tail = mix(hn_ref[...], tile_prev)
        o_ref[...] += y
        finish(tail)


def _conv_mixer(x, gain, w_in, conv_w, w_out, tail_in, *, tm, nc, tiles_per_seq, side_casts=()):
    t, d = x.shape
    nt, nj = t // tm, d // nc
    side_in, side_out, side_shapes, side_vmem = _side_cast_specs(side_casts, nt, nj)
    est = (2 * 2 * tm * d * 4 + tm * d * 2 + 2 * 4 * d * nc * 2 + 8 * tm * nc * 4) + side_vmem
    kern = functools.partial(_conv_mixer_kernel, tiles_per_seq=tiles_per_seq,
                             side_fns=tuple(s.fn for s in side_casts))
    return pl.pallas_call(
        kern,
        grid=(nt, nj),
        in_specs=[
            pl.BlockSpec((tm, d), lambda i, j: (i, 0)),
            pl.BlockSpec((1, d), lambda i, j: (0, 0)),
            pl.BlockSpec((d, nc), lambda i, j: (0, j)),
            pl.BlockSpec((d, nc), lambda i, j: (0, nj + j)),
            pl.BlockSpec((d, nc), lambda i, j: (0, 2 * nj + j)),
            pl.BlockSpec((CONV_WIDTH, nc), lambda i, j: (0, j)),
            pl.BlockSpec((nc, d), lambda i, j: (j, 0)),
            pl.BlockSpec((V7X_SUBLANES, nc), lambda i, j: (0, j)),
        ] + side_in,
        out_specs=[
            pl.BlockSpec((tm, d), lambda i, j: (i, 0)),
            pl.BlockSpec((V7X_SUBLANES, nc), lambda i, j: (i, j)),
        ] + side_out,
        out_shape=[
            jax.ShapeDtypeStruct((t, d), F32),
            jax.ShapeDtypeStruct((nt * V7X_SUBLANES, d), F32),
        ] + side_shapes,
        scratch_shapes=[
            pltpu.VMEM((tm, d), BF16),
            pltpu.VMEM((nj, V7X_SUBLANES, nc), F32),
        ],
        compiler_params=_params(est),
        name="conv_mixer",
    )(x, gain, w_in, w_in, w_in, conv_w, w_out, tail_in, *(s.w for s in side_casts))


def _mlp_kernel(*refs, final_norm, side_fns):
    n = len(side_fns)
    x_ref, g_ref, wu_ref, wd_ref, gf_ref = refs[:5]
    o_ref = refs[5 + n]
    hn_ref = refs[6 + 2 * n]
    j = pl.program_id(1)
    tm = hn_ref.shape[0]
    rm = tm // _row_splits(tm, NORM_ROW_SPLITS)

    def ffn(hn):
        a = jnp.maximum(_dot(hn, wu_ref[...]), 0.0)
        return _dot((a * a).astype(BF16), wd_ref[...])

    side_casts = functools.partial(_run_side_casts, side_fns, refs[5:5 + n], refs[6 + n:6 + 2 * n])

    @pl.when(j == 0)
    def _():
        for r0 in range(0, tm, rm):
            rows = slice(r0, r0 + rm)
            x = x_ref[rows, :]
            hn = _rms_norm(x, g_ref[...]).astype(BF16)
            hn_ref[rows, :] = hn
            o_ref[rows, :] = x + ffn(hn)
        side_casts()

    @pl.when(j > 0)
    def _():
        o_ref[...] += ffn(hn_ref[...])
        side_casts()

    if final_norm:
        @pl.when(j == pl.num_programs(1) - 1)
        def _():
            o_ref[...] = _rms_norm(o_ref[...], gf_ref[...])


def _mlp(x, gain, w_up, w_down, gain_final, *, tm, fc, final_norm, side_casts=()):
    t, d = x.shape
    f = w_up.shape[1]
    nt, nj = t // tm, f // fc
    side_in, side_out, side_shapes, side_vmem = _side_cast_specs(side_casts, nt, nj)
    est = 2 * 2 * tm * d * 4 + tm * d * 2 + 2 * 2 * d * fc * 2 + 3 * tm * fc * 4 + side_vmem
    kern = functools.partial(_mlp_kernel, final_norm=final_norm, side_fns=tuple(s.fn for s in side_casts))
    return pl.pallas_call(
        kern,
        grid=(nt, nj),
        in_specs=[
            pl.BlockSpec((tm, d), lambda i, j: (i, 0)),
            pl.BlockSpec((1, d), lambda i, j: (0, 0)),
            pl.BlockSpec((d, fc), lambda i, j: (0, j)),
            pl.BlockSpec((fc, d), lambda i, j: (j, 0)),
            pl.BlockSpec((1, d), lambda i, j: (0, 0)),
        ] + side_in,
        out_specs=[pl.BlockSpec((tm, d), lambda i, j: (i, 0))] + side_out,
        out_shape=[jax.ShapeDtypeStruct((t, d), F32)] + side_shapes,
        scratch_shapes=[pltpu.VMEM((tm, d), BF16)],
        compiler_params=_params(est),
        name="mlp_final" if final_norm else "mlp",
    )(x, gain, w_up, w_down, gain_final, *(s.w for s in side_casts))


def _qkv_kernel(*refs, v_chunk, side_fns):
    n = len(side_fns)
    x_ref, g_ref, w_ref, cos_ref, sin_ref = refs[:5]
    o_ref = refs[5 + n]
    hn_ref = refs[6 + 2 * n]
    side_casts = functools.partial(_run_side_casts, side_fns, refs[5:5 + n], refs[6 + n:6 + 2 * n])
    j = pl.program_id(1)
    is_v = j == v_chunk
    lane = lax.broadcasted_iota(jnp.int32, (1, V7X_LANES), 1)
    first_half = (lane % HEAD_DIM) < HALF_DIM
    tm = hn_ref.shape[0]
    rm = tm // _row_splits(tm, QKV_ROW_SPLITS)

    def project(rows, hn):
        y = _dot(hn, w_ref[...])
        cos = jnp.where(is_v, 1.0, cos_ref[rows, :])
        sin = jnp.where(is_v, 0.0, sin_ref[rows, :])
        for c in range(y.shape[1] // V7X_LANES):
            sl = slice(c * V7X_LANES, (c + 1) * V7X_LANES)
            t = y[:, sl]
            rot = jnp.where(first_half,
                            pltpu.roll(t, V7X_LANES - HALF_DIM, axis=1),
                            pltpu.roll(t, HALF_DIM, axis=1))
            o_ref[rows, sl] = (t * cos + rot * sin).astype(BF16)

    @pl.when(j == 0)
    def _():
        for r0 in range(0, tm, rm):
            rows = slice(r0, r0 + rm)
            hn = _rms_norm(x_ref[rows, :], g_ref[...]).astype(BF16)
            hn_ref[rows, :] = hn
            project(rows, hn)
        side_casts()

    @pl.when(j > 0)
    def _():
        for r0 in range(0, tm, rm):
            rows = slice(r0, r0 + rm)
            project(rows, hn_ref[rows, :])
        side_casts()


def _qkv(x, gain, w_ext, cos_t, sin_t, *, tm, nc, tiles_per_seq, side_casts=()):
    t, d = x.shape
    nw = w_ext.shape[1]
    nt, nj = t // tm, nw // nc
    side_in, side_out, side_shapes, side_vmem = _side_cast_specs(side_casts, nt, nj)
    est = (2 * tm * d * 4 + tm * d * 2 + 2 * d * nc * 2 + 2 * 2 * tm * 128 * 4 + 2 * tm * nc * 2
           + 4 * tm * nc * 4 + side_vmem)
    kern = functools.partial(_qkv_kernel, v_chunk=nj - 1, side_fns=tuple(s.fn for s in side_casts))
    return pl.pallas_call(
        kern,
        grid=(nt, nj),
        in_specs=[
            pl.BlockSpec((tm, d), lambda i, j: (i, 0)),
            pl.BlockSpec((1, d), lambda i, j: (0, 0)),
            pl.BlockSpec((d, nc), lambda i, j: (0, j)),
            pl.BlockSpec((tm, V7X_LANES), lambda i, j: (i % tiles_per_seq, 0)),
            pl.BlockSpec((tm, V7X_LANES), lambda i, j: (i % tiles_per_seq, 0)),
        ] + side_in,
        out_specs=[pl.BlockSpec((tm, nc), lambda i, j: (i, j))] + side_out,
        out_shape=[jax.ShapeDtypeStruct((t, nw), BF16)] + side_shapes,
        scratch_shapes=[pltpu.VMEM((tm, d), BF16)],
        compiler_params=_params(est),
        name="qkv_rope",
    )(x, gain, w_ext, cos_t, sin_t, *(s.w for s in side_casts))


def _attn_kernel(sink_ref, q_ref, kvc_ref, kvp_ref, kvm_ref, o_ref, *, n_kv, pad):
    n = pl.program_id(1)
    blk = ATTN_BLOCK
    kv_prev = jnp.where(n == 0, kvm_ref[...], kvp_ref[...])
    band = jnp.concatenate([kv_prev, kvc_ref[...]], axis=0)

    qi = lax.broadcasted_iota(jnp.int32, (blk, 2 * blk), 0)
    kj = lax.broadcasted_iota(jnp.int32, (blk, 2 * blk), 1)
    allowed = (kj > qi) & (kj <= qi + blk) & (kj + n * blk >= pad)

    sink_slot = lax.broadcasted_iota(jnp.int32, (1, 2 * blk), 1) == 0
    low = lax.broadcasted_iota(jnp.int32, (2 * blk, V7X_LANES), 1) < HEAD_DIM
    not_sink_row = lax.broadcasted_iota(jnp.int32, (V7X_BF16_ROWS, V7X_LANES), 0) > 0
    zero = jnp.zeros((), BF16)
    low_row = lax.broadcasted_iota(jnp.int32, (1, V7X_LANES), 1) < HEAD_DIM
    ones_low = jnp.broadcast_to(jnp.where(low_row, 1.0, 0.0).astype(BF16), (2 * blk, V7X_LANES))
    ones_high = jnp.broadcast_to(jnp.where(low_row, 0.0, 1.0).astype(BF16), (2 * blk, V7X_LANES))
    pairs = 4
    contract_lanes = (((1,), (1,)), ((), ()))

    for h in range(n_kv):
        kd = band[:, h * V7X_LANES:(h + 1) * V7X_LANES]
        vd = band[:, (n_kv + h) * V7X_LANES:(n_kv + h + 1) * V7X_LANES]
        sink_tile = jnp.where(not_sink_row, vd[:V7X_BF16_ROWS].astype(F32), 0.0).astype(BF16)
        vd = jnp.concatenate([sink_tile, vd[V7X_BF16_ROWS:]], axis=0)
        k_sides = (jnp.where(low, kd, zero), jnp.where(low, zero, kd))
        values = jnp.concatenate([
            jnp.concatenate([jnp.where(low, vd, zero), ones_low], axis=1),
            jnp.concatenate([jnp.where(low, zero, vd), ones_high], axis=1)], axis=0)
        qbase = h * pairs * V7X_LANES
        qst = jnp.concatenate(
            [q_ref[:, qbase + p * V7X_LANES: qbase + (p + 1) * V7X_LANES] for p in range(pairs)],
            axis=0)
        e_sides = []
        for side in range(2):
            s_all = lax.dot_general(qst, k_sides[side], contract_lanes, preferred_element_type=F32)
            es = []
            for p in range(pairs):
                fill = jnp.where(sink_slot, sink_ref[h * 2 * pairs + 2 * p + side], NEG_INF)
                s = jnp.where(allowed, s_all[p * blk:(p + 1) * blk], fill)
                e = jnp.exp(s - jnp.max(s, axis=-1, keepdims=True))
                es.append(e.astype(BF16))
            e_sides.append(jnp.concatenate(es, axis=0))
        o_den = _dot(jnp.concatenate(e_sides, axis=1), values)
        out = o_den[:, :V7X_LANES] * (1.0 / o_den[:, V7X_LANES:])
        for p in range(pairs):
            o_ref[:, qbase + p * V7X_LANES: qbase + (p + 1) * V7X_LANES] = (
                out[p * blk:(p + 1) * blk].astype(BF16))


def _attention(sinks, qkv, kv_meta, *, d, batch, n_kv, pad):
    t = qkv.shape[0]
    blk = ATTN_BLOCK
    nb = t // batch // blk
    kvw = kv_meta.shape[1]
    kv_col = d // kvw
    assert kv_col * kvw == d and qkv.shape[1] == d + kvw
    est = 2 * 2 * blk * d * 2 + 3 * 2 * blk * kvw * 2 + 24 * 512 * 256 * 4
    return pl.pallas_call(
        functools.partial(_attn_kernel, n_kv=n_kv, pad=pad),
        grid=(batch, nb),
        in_specs=[
            pl.BlockSpec(memory_space=pltpu.SMEM),
            pl.BlockSpec((blk, d), lambda b, n: (b * nb + n, 0)),
            pl.BlockSpec((blk, kvw), lambda b, n: (b * nb + n, kv_col)),
            pl.BlockSpec((blk, kvw), lambda b, n: (b * nb + jnp.maximum(n - 1, 0), kv_col)),
            pl.BlockSpec((blk, kvw), lambda b, n: (0, 0)),
        ],
        out_specs=pl.BlockSpec((blk, d), lambda b, n: (b * nb + n, 0)),
        out_shape=jax.ShapeDtypeStruct((t, d), BF16),
        compiler_params=_params(est),
        name="swa_attention",
    )(sinks, qkv, qkv, qkv, kv_meta)


def _out_proj_kernel(a_ref, w_ref, h_ref, o_ref):
    o_ref[...] = h_ref[...] + _dot(a_ref[...], w_ref[...])


def _out_proj(a, w, h, *, tm):
    t, d = h.shape
    k = a.shape[1]
    est = 2 * tm * k * 2 + 2 * k * d * 2 + 2 * 2 * tm * d * 4 + tm * d * 4
    return pl.pallas_call(
        _out_proj_kernel,
        grid=(t // tm, 1),
        in_specs=[
            pl.BlockSpec((tm, k), lambda i, j: (i, 0)),
            pl.BlockSpec((k, d), lambda i, j: (0, 0)),
            pl.BlockSpec((tm, d), lambda i, j: (i, 0)),
        ],
        out_specs=pl.BlockSpec((tm, d), lambda i, j: (i, 0)),
        out_shape=jax.ShapeDtypeStruct((t, d), F32),
        compiler_params=_params(est),
        name="attn_out_proj",
    )(a, w, h)


def _rope_tables(first_pos, n_pos):
    pos = jnp.arange(first_pos, first_pos + n_pos, dtype=F32)
    inv = ROPE_THETA ** (-jnp.arange(0, HEAD_DIM, 2, dtype=F32) / HEAD_DIM)
    ang = pos[:, None] * inv[None, :]
    cos, sin = jnp.cos(ang), jnp.sin(ang)
    return (jnp.concatenate([cos, cos, cos, cos], axis=1),
            jnp.concatenate([-sin, sin, -sin, sin], axis=1))


def _extend_qkv_block(blk, *, d):
    low = lax.broadcasted_iota(jnp.int32, (1, V7X_LANES), 1) < HEAD_DIM
    pieces = [(blk[:, :d] * HEAD_DIM ** -0.5).astype(BF16)]
    for c0 in range(d, blk.shape[1], V7X_LANES):
        t = blk[:, c0:c0 + V7X_LANES]
        swapped = pltpu.roll(t, HEAD_DIM, axis=1)
        pieces += [jnp.where(low, t, swapped).astype(BF16), jnp.where(low, swapped, t).astype(BF16)]
    return jnp.concatenate(pieces, axis=1)


def kernel(x, meta_tokens, norm_mix_0, w_in_conv, conv_w, w_out_conv, norm_mlp_0, w_up_0, w_down_0,
           norm_mix_1, w_qkv, attn_sinks, w_o, norm_mlp_1, w_up_1, w_down_1, norm_final):
    batch, seq, d = x.shape
    n_meta = meta_tokens.shape[0]
    n_kv = (w_qkv.shape[1] - d) // (2 * HEAD_DIM)
    pad = (-(seq + n_meta)) % ATTN_BLOCK
    assert seq % ATTN_BLOCK == 0 and pad + n_meta == ATTN_BLOCK
    assert n_meta % (2 * V7X_SUBLANES) == 0 and d // HEAD_DIM == 8 * n_kv

    tm, cc, nc, fc = TOKEN_TILE, CONV_CHUNK, COL_CHUNK, FF_CHUNK
    assert seq % tm == 0 and d % nc == 0
    tiles_per_seq = seq // tm
    n_q_chunks = d // nc

    row = lambda g: g.reshape(1, d)
    g_mix0, g_mlp0, g_mix1, g_mlp1, g_fin = map(row, (norm_mix_0, norm_mlp_0, norm_mix_1, norm_mlp_1, norm_final))
    kw = n_kv * HEAD_DIM
    assert 2 * kw == nc, "duplicated k heads (and v heads) must each fill exactly one column chunk"
    cos_m, sin_m = _rope_tables(0, n_meta)
    cos_t, sin_t = _rope_tables(n_meta, seq)
    w_in, w_out = w_in_conv.astype(BF16), w_out_conv.astype(BF16)

    zeros_tail = jnp.zeros((V7X_SUBLANES, d), F32)
    hm, tail_m = _conv_mixer(meta_tokens, g_mix0, w_in, conv_w, w_out, zeros_tail,
                             tm=n_meta, nc=cc, tiles_per_seq=1)
    h = x.reshape(batch * seq, d)
    assert seq % CONV_TILE == 0
    qkv_cast = _SideCast(w_qkv, d + 4 * kw, functools.partial(_extend_qkv_block, d=d))
    h, _, wu0, wd0, w_ext, wo = _conv_mixer(
        h, g_mix0, w_in, conv_w, w_out, tail_m[-V7X_SUBLANES:],
        tm=CONV_TILE, nc=cc, tiles_per_seq=seq // CONV_TILE,
        side_casts=(_plain_cast(w_up_0), _plain_cast(w_down_0), qkv_cast, _plain_cast(w_o)))

    (hm,) = _mlp(hm, g_mlp0, wu0, wd0, g_fin, tm=n_meta, fc=fc, final_norm=False)
    (h,) = _mlp(h, g_mlp0, wu0, wd0, g_fin, tm=tm, fc=fc, final_norm=False)

    (qkv_m,) = _qkv(hm, g_mix1, w_ext, cos_m, sin_m, tm=n_meta, nc=nc, tiles_per_seq=1)
    kv_meta = jnp.pad(qkv_m[:, d:], ((pad, 0), (0, 0)))
    qkv, wu1, wd1 = _qkv(h, g_mix1, w_ext, cos_t, sin_t, tm=tm, nc=nc, tiles_per_seq=tiles_per_seq,
                         side_casts=(_plain_cast(w_up_1), _plain_cast(w_down_1)))
    a = _attention(attn_sinks, qkv, kv_meta, d=d, batch=batch, n_kv=n_kv, pad=pad)
    h = _out_proj(a, wo, h, tm=OUT_PROJ_TILE)
    (out,) = _mlp(h, g_mlp1, wu1, wd1, g_fin, tm=tm, fc=fc, final_norm=True)
    return out.reshape(batch, seq, d)
```

```python
import functools
from typing import Callable, NamedTuple

import jax
import jax.numpy as jnp
from jax import lax
from jax.experimental import pallas as pl
from jax.experimental.pallas import tpu as pltpu

HEAD_DIM = 64
HALF_DIM = HEAD_DIM // 2
ATTN_BLOCK = 128
ROPE_THETA = 10000.0
RMS_EPS = 1e-5
NEG_INF = -1e30
CONV_WIDTH = 3

V7X_LANES = 128
V7X_SUBLANES = 8
V7X_BF16_ROWS = 16
V7X_VMEM_LIMIT_BYTES = 60000 * 1024

TOKEN_TILE = 1024
CONV_TILE = 1024
OUT_PROJ_TILE = 512
CONV_CHUNK = 512
COL_CHUNK = 512
FF_CHUNK = 1024
QKV_ROW_SPLITS = 8
CONV_NORM_ROW_SPLITS = 2
NORM_ROW_SPLITS = 8

BF16 = jnp.bfloat16
F32 = jnp.float32


def _dot(a, b):
    return jnp.dot(a, b, preferred_element_type=F32)


def _rms_norm(x, g):
    var = jnp.mean(x * x, axis=-1, keepdims=True)
    return (x * lax.rsqrt(var + RMS_EPS)) * g


def _row_splits(tm, splits):
    return splits if tm % (splits * 128) == 0 else 1


def _params(est_bytes):
    limit = min(V7X_VMEM_LIMIT_BYTES, max(int(est_bytes * 1.25), 16 * 1024 * 1024))
    return pltpu.CompilerParams(
        dimension_semantics=("arbitrary", "arbitrary"), vmem_limit_bytes=limit)


class _SideCast(NamedTuple):
    w: jax.Array
    out_cols: int
    fn: Callable


def _plain_cast(w):
    return _SideCast(w, w.shape[1], lambda blk: blk.astype(BF16))


def _side_cast_specs(sides, nt, nj):
    n_steps = nt * nj
    in_specs, out_specs, out_shapes, vmem = [], [], [], 0
    for s in sides:
        rows, cols = s.w.shape
        n_blocks = 1
        while 2 * n_blocks <= min(n_steps, rows // V7X_BF16_ROWS) and rows % (2 * n_blocks) == 0:
            n_blocks *= 2
        rb = rows // n_blocks
        assert rb % V7X_BF16_ROWS == 0
        index = lambda i, j, nb=n_blocks: (((i * nj + j) * nb) // n_steps, 0)
        in_specs.append(pl.BlockSpec((rb, cols), index))
        out_specs.append(pl.BlockSpec((rb, s.out_cols), index))
        out_shapes.append(jax.ShapeDtypeStruct((rows, s.out_cols), BF16))
        vmem += 2 * rb * (cols * 4 + s.out_cols * 2)
    return in_specs, out_specs, out_shapes, vmem


def _run_side_casts(fns, in_refs, out_refs):
    for fn, w_ref, o_ref in zip(fns, in_refs, out_refs):
        o_ref[...] = fn(w_ref[...])


class _TileStream:
    def __init__(self, x_hbm, o_hbm, buf, in_sem, out_sem):
        self.x_hbm, self.o_hbm, self.buf, self.in_sem, self.out_sem = x_hbm, o_hbm, buf, in_sem, out_sem
        self.i, self.j = pl.program_id(0), pl.program_id(1)
        self.nt, self.nj = pl.num_programs(0), pl.num_programs(1)
        self.tm = buf.shape[1]
        self.slot = self.i % 2
        self.tile = buf.at[self.slot]

    def _rows(self, tile):
        return pl.ds(pl.multiple_of(tile * self.tm, self.tm), self.tm)

    def _arrival(self, tile, slot):
        return pltpu.make_async_copy(self.x_hbm.at[self._rows(tile), :], self.buf.at[slot], self.in_sem.at[slot])

    def _departure(self, tile, slot):
        return pltpu.make_async_copy(self.buf.at[slot], self.o_hbm.at[self._rows(tile), :], self.out_sem.at[slot])

    def begin_step(self, n_steps):
        i, j, other = self.i, self.j, 1 - self.slot

        @pl.when(j == 0)
        def _():
            @pl.when(i == 0)
            def _():
                self._arrival(0, 0).start()
            self._arrival(i, self.slot).wait()

        @pl.when(j == min(1, n_steps - 1))
        def _():
            @pl.when(i > 0)
            def _():
                self._departure(i - 1, other).wait()

            @pl.when(i + 1 < self.nt)
            def _():
                self._arrival(i + 1, other).start()

    def end_step(self, n_steps):
        i = self.i

        @pl.when(self.j == n_steps - 1)
        def _():
            self._departure(i, self.slot).start()

            @pl.when(i == self.nt - 1)
            def _():
                self._departure(i, self.slot).wait()


def _conv_mixer_kernel(*refs, tiles_per_seq, n_steps, side_fns):
    n = len(side_fns)
    x_hbm, g_ref, wb_ref, wc_ref, wu_ref, cw_ref, wo_ref, tail_in_ref = refs[:8]
    o_hbm, tail_out_ref = refs[8 + n:10 + n]
    hn_ref, carry_ref, buf, in_sem, out_sem = refs[10 + 2 * n:]
    stream = _TileStream(x_hbm, o_hbm, buf, in_sem, out_sem)
    stream.begin_step(n_steps)
    o_ref = stream.tile
    i = pl.program_id(0)
    j = pl.program_id(1)
    tm = hn_ref.shape[0]
    cw = cw_ref[...]
    first_tile_of_seq = (i % tiles_per_seq) == 0
    tile_prev = jnp.where(first_tile_of_seq, tail_in_ref[...], carry_ref[j])

    def mix(hn, prev):
        gate_b = _dot(hn, wb_ref[...])
        v = _dot(hn, wc_ref[...]) * _dot(hn, wu_ref[...])
        vcat = jnp.concatenate([prev, v], axis=0)
        v_m1 = pltpu.roll(vcat, 1, axis=0)[V7X_SUBLANES:]
        v_m2 = pltpu.roll(vcat, 2, axis=0)[V7X_SUBLANES:]
        conv = cw[0:1] * v_m2 + cw[1:2] * v_m1 + cw[2:3] * v
        gated = (gate_b * conv).astype(BF16)
        return _dot(gated, wo_ref[...]), v[v.shape[0] - V7X_SUBLANES:]

    def finish(tail):
        carry_ref[j] = tail
        tail_out_ref[...] = tail
        _run_side_casts(side_fns, refs[8:8 + n], refs[10 + n:10 + 2 * n])

    @pl.when(j == 0)
    def _():
        rm = tm // _row_splits(tm, CONV_NORM_ROW_SPLITS)
        prev = tile_prev
        for r0 in range(0, tm, rm):
            rows = slice(r0, r0 + rm)
            x = o_ref[rows, :]
            hn = _rms_norm(x, g_ref[...]).astype(BF16)
            hn_ref[rows, :] = hn
            y, prev = mix(hn, prev)
            o_ref[rows, :] = x + y
        finish(prev)

    @pl.when(j > 0)
    def _():
        y, tail = mix(hn_ref[...], tile_prev)
        o_ref[...] += y
        finish(tail)

    stream.end_step(n_steps)


def _conv_mixer(x, gain, w_in, conv_w, w_out, tail_in, *, tm, nc, tiles_per_seq, side_casts=()):
    t, d = x.shape
    nt, nj = t // tm, d // nc
    side_in, side_out, side_shapes, side_vmem = _side_cast_specs(side_casts, nt, nj)
    est = (2 * tm * d * 4 + tm * d * 2 + 2 * 4 * d * nc * 2 + 8 * tm * nc * 4) + side_vmem
    kern = functools.partial(_conv_mixer_kernel, tiles_per_seq=tiles_per_seq, n_steps=nj,
                             side_fns=tuple(s.fn for s in side_casts))
    return pl.pallas_call(
        kern,
        grid=(nt, nj),
        in_specs=[
            pl.BlockSpec(memory_space=pl.ANY),
            pl.BlockSpec((1, d), lambda i, j: (0, 0)),
            pl.BlockSpec((d, nc), lambda i, j: (0, j)),
            pl.BlockSpec((d, nc), lambda i, j: (0, nj + j)),
            pl.BlockSpec((d, nc), lambda i, j: (0, 2 * nj + j)),
            pl.BlockSpec((CONV_WIDTH, nc), lambda i, j: (0, j)),
            pl.BlockSpec((nc, d), lambda i, j: (j, 0)),
            pl.BlockSpec((V7X_SUBLANES, nc), lambda i, j: (0, j)),
        ] + side_in,
        out_specs=[
            pl.BlockSpec(memory_space=pl.ANY),
            pl.BlockSpec((V7X_SUBLANES, nc), lambda i, j: (i, j)),
        ] + side_out,
        out_shape=[
            jax.ShapeDtypeStruct((t, d), F32),
            jax.ShapeDtypeStruct((nt * V7X_SUBLANES, d), F32),
        ] + side_shapes,
        scratch_shapes=[
            pltpu.VMEM((tm, d), BF16),
            pltpu.VMEM((nj, V7X_SUBLANES, nc), F32),
            pltpu.VMEM((2, tm, d), F32),
            pltpu.SemaphoreType.DMA((2,)),
            pltpu.SemaphoreType.DMA((2,)),
        ],
        compiler_params=_params(est),
        name="conv_mixer",
    )(x, gain, w_in, w_in, w_in, conv_w, w_out, tail_in, *(s.w for s in side_casts))


def _mlp_kernel(*refs, final_norm, side_fns):
    n = len(side_fns)
    x_ref, g_ref, wu_ref, wd_ref, gf_ref = refs[:5]
    o_ref = refs[5 + n]
    hn_ref = refs[6 + 2 * n]
    j = pl.program_id(1)
    tm = hn_ref.shape[0]
    rm = tm // _row_splits(tm, NORM_ROW_SPLITS)

    def ffn(hn):
        a = jnp.maximum(_dot(hn, wu_ref[...]), 0.0)
        return _dot((a * a).astype(BF16), wd_ref[...])

    side_casts = functools.partial(_run_side_casts, side_fns, refs[5:5 + n], refs[6 + n:6 + 2 * n])

    @pl.when(j == 0)
    def _():
        for r0 in range(0, tm, rm):
            rows = slice(r0, r0 + rm)
            x = x_ref[rows, :]
            hn = _rms_norm(x, g_ref[...]).astype(BF16)
            hn_ref[rows, :] = hn
            o_ref[rows, :] = x + ffn(hn)
        side_casts()

    @pl.when(j > 0)
    def _():
        o_ref[...] += ffn(hn_ref[...])
        side_casts()

    if final_norm:
        @pl.when(j == pl.num_programs(1) - 1)
        def _():
            o_ref[...] = _rms_norm(o_ref[...], gf_ref[...])


def _mlp(x, gain, w_up, w_down, gain_final, *, tm, fc, final_norm, side_casts=()):
    t, d = x.shape
    f = w_up.shape[1]
    nt, nj = t // tm, f // fc
    side_in, side_out, side_shapes, side_vmem = _side_cast_specs(side_casts, nt, nj)
    est = 2 * 2 * tm * d * 4 + tm * d * 2 + 2 * 2 * d * fc * 2 + 3 * tm * fc * 4 + side_vmem
    kern = functools.partial(_mlp_kernel, final_norm=final_norm, side_fns=tuple(s.fn for s in side_casts))
    return pl.pallas_call(
        kern,
        grid=(nt, nj),
        in_specs=[
            pl.BlockSpec((tm, d), lambda i, j: (i, 0)),
            pl.BlockSpec((1, d), lambda i, j: (0, 0)),
            pl.BlockSpec((d, fc), lambda i, j: (0, j)),
            pl.BlockSpec((fc, d), lambda i, j: (j, 0)),
            pl.BlockSpec((1, d), lambda i, j: (0, 0)),
        ] + side_in,
        out_specs=[pl.BlockSpec((tm, d), lambda i, j: (i, 0))] + side_out,
        out_shape=[jax.ShapeDtypeStruct((t, d), F32)] + side_shapes,
        scratch_shapes=[pltpu.VMEM((tm, d), BF16)],
        compiler_params=_params(est),
        name="mlp_final" if final_norm else "mlp",
    )(x, gain, w_up, w_down, gain_final, *(s.w for s in side_casts))


def _qkv_kernel(*refs, v_chunk, side_fns):
    n = len(side_fns)
    x_ref, g_ref, w_ref, cos_ref, sin_ref = refs[:5]
    o_ref = refs[5 + n]
    hn_ref = refs[6 + 2 * n]
    side_casts = functools.partial(_run_side_casts, side_fns, refs[5:5 + n], refs[6 + n:6 + 2 * n])
    j = pl.program_id(1)
    is_v = j == v_chunk
    lane = lax.broadcasted_iota(jnp.int32, (1, V7X_LANES), 1)
    first_half = (lane % HEAD_DIM) < HALF_DIM
    tm = hn_ref.shape[0]
    rm = tm // _row_splits(tm, QKV_ROW_SPLITS)

    def project(rows, hn):
        y = _dot(hn, w_ref[...])
        cos = jnp.where(is_v, 1.0, cos_ref[rows, :])
        sin = jnp.where(is_v, 0.0, sin_ref[rows, :])
        for c in range(y.shape[1] // V7X_LANES):
            sl = slice(c * V7X_LANES, (c + 1) * V7X_LANES)
            t = y[:, sl]
            rot = jnp.where(first_half,
                            pltpu.roll(t, V7X_LANES - HALF_DIM, axis=1),
                            pltpu.roll(t, HALF_DIM, axis=1))
            o_ref[rows, sl] = (t * cos + rot * sin).astype(BF16)

    @pl.when(j == 0)
    def _():
        for r0 in range(0, tm, rm):
            rows = slice(r0, r0 + rm)
            hn = _rms_norm(x_ref[rows, :], g_ref[...]).astype(BF16)
            hn_ref[rows, :] = hn
            project(rows, hn)
        side_casts()

    @pl.when(j > 0)
    def _():
        for r0 in range(0, tm, rm):
            rows = slice(r0, r0 + rm)
            project(rows, hn_ref[rows, :])
        side_casts()


def _qkv(x, gain, w_ext, cos_t, sin_t, *, tm, nc, tiles_per_seq, side_casts=()):
    t, d = x.shape
    nw = w_ext.shape[1]
    nt, nj = t // tm, nw // nc
    side_in, side_out, side_shapes, side_vmem = _side_cast_specs(side_casts, nt, nj)
    est = (2 * tm * d * 4 + tm * d * 2 + 2 * d * nc * 2 + 2 * 2 * tm * 128 * 4 + 2 * tm * nc * 2
           + 4 * tm * nc * 4 + side_vmem)
    kern = functools.partial(_qkv_kernel, v_chunk=nj - 1, side_fns=tuple(s.fn for s in side_casts))
    return pl.pallas_call(
        kern,
        grid=(nt, nj),
        in_specs=[
            pl.BlockSpec((tm, d), lambda i, j: (i, 0)),
            pl.BlockSpec((1, d), lambda i, j: (0, 0)),
            pl.BlockSpec((d, nc), lambda i, j: (0, j)),
            pl.BlockSpec((tm, V7X_LANES), lambda i, j: (i % tiles_per_seq, 0)),
            pl.BlockSpec((tm, V7X_LANES), lambda i, j: (i % tiles_per_seq, 0)),
        ] + side_in,
        out_specs=[pl.BlockSpec((tm, nc), lambda i, j: (i, j))] + side_out,
        out_shape=[jax.ShapeDtypeStruct((t, nw), BF16)] + side_shapes,
        scratch_shapes=[pltpu.VMEM((tm, d), BF16)],
        compiler_params=_params(est),
        name="qkv_rope",
    )(x, gain, w_ext, cos_t, sin_t, *(s.w for s in side_casts))


def _attn_kernel(*refs, n_kv, pad, side_fns):
    ns = len(side_fns)
    sink_ref, q_ref, kvc_ref, kvp_ref, kvm_ref = refs[:5]
    o_ref = refs[5 + ns]
    _run_side_casts(side_fns, refs[5:5 + ns], refs[6 + ns:6 + 2 * ns])
    n = pl.program_id(1)
    blk = ATTN_BLOCK
    kv_prev = jnp.where(n == 0, kvm_ref[...], kvp_ref[...])
    band = jnp.concatenate([kv_prev, kvc_ref[...]], axis=0)

    qi = lax.broadcasted_iota(jnp.int32, (blk, 2 * blk), 0)
    kj = lax.broadcasted_iota(jnp.int32, (blk, 2 * blk), 1)
    allowed = (kj > qi) & (kj <= qi + blk) & (kj + n * blk >= pad)

    sink_slot = lax.broadcasted_iota(jnp.int32, (1, 2 * blk), 1) == 0
    low = lax.broadcasted_iota(jnp.int32, (2 * blk, V7X_LANES), 1) < HEAD_DIM
    not_sink_row = lax.broadcasted_iota(jnp.int32, (V7X_BF16_ROWS, V7X_LANES), 0) > 0
    zero = jnp.zeros((), BF16)
    low_row = lax.broadcasted_iota(jnp.int32, (1, V7X_LANES), 1) < HEAD_DIM
    ones_low = jnp.broadcast_to(jnp.where(low_row, 1.0, 0.0).astype(BF16), (2 * blk, V7X_LANES))
    ones_high = jnp.broadcast_to(jnp.where(low_row, 0.0, 1.0).astype(BF16), (2 * blk, V7X_LANES))
    pairs = 4
    contract_lanes = (((1,), (1,)), ((), ()))

    for h in range(n_kv):
        kd = band[:, h * V7X_LANES:(h + 1) * V7X_LANES]
        vd = band[:, (n_kv + h) * V7X_LANES:(n_kv + h + 1) * V7X_LANES]
        sink_tile = jnp.where(not_sink_row, vd[:V7X_BF16_ROWS].astype(F32), 0.0).astype(BF16)
        vd = jnp.concatenate([sink_tile, vd[V7X_BF16_ROWS:]], axis=0)
        k_sides = (jnp.where(low, kd, zero), jnp.where(low, zero, kd))
        values = jnp.concatenate([
            jnp.concatenate([jnp.where(low, vd, zero), ones_low], axis=1),
            jnp.concatenate([jnp.where(low, zero, vd), ones_high], axis=1)], axis=0)
        qbase = h * pairs * V7X_LANES
        qst = jnp.concatenate(
            [q_ref[:, qbase + p * V7X_LANES: qbase + (p + 1) * V7X_LANES] for p in range(pairs)],
            axis=0)
        e_sides = []
        for side in range(2):
            s_all = lax.dot_general(qst, k_sides[side], contract_lanes, preferred_element_type=F32)
            es = []
            for p in range(pairs):
                fill = jnp.where(sink_slot, sink_ref[h * 2 * pairs + 2 * p + side], NEG_INF)
                s = jnp.where(allowed, s_all[p * blk:(p + 1) * blk], fill)
                e = jnp.exp(s - jnp.max(s, axis=-1, keepdims=True))
                es.append(e.astype(BF16))
            e_sides.append(jnp.concatenate(es, axis=0))
        o_den = _dot(jnp.concatenate(e_sides, axis=1), values)
        out = o_den[:, :V7X_LANES] * (1.0 / o_den[:, V7X_LANES:])
        for p in range(pairs):
            o_ref[:, qbase + p * V7X_LANES: qbase + (p + 1) * V7X_LANES] = (
                out[p * blk:(p + 1) * blk].astype(BF16))


def _attention(sinks, qkv, kv_meta, *, d, batch, n_kv, pad, side_casts=()):
    t = qkv.shape[0]
    blk = ATTN_BLOCK
    nb = t // batch // blk
    kvw = kv_meta.shape[1]
    kv_col = d // kvw
    assert kv_col * kvw == d and qkv.shape[1] == d + kvw
    side_in, side_out, side_shapes, side_vmem = _side_cast_specs(side_casts, batch, nb)
    est = 2 * 2 * blk * d * 2 + 3 * 2 * blk * kvw * 2 + 24 * 512 * 256 * 4 + side_vmem
    kern = functools.partial(_attn_kernel, n_kv=n_kv, pad=pad, side_fns=tuple(s.fn for s in side_casts))
    return pl.pallas_call(
        kern,
        grid=(batch, nb),
        in_specs=[
            pl.BlockSpec(memory_space=pltpu.SMEM),
            pl.BlockSpec((blk, d), lambda b, n: (b * nb + n, 0)),
            pl.BlockSpec((blk, kvw), lambda b, n: (b * nb + n, kv_col)),
            pl.BlockSpec((blk, kvw), lambda b, n: (b * nb + jnp.maximum(n - 1, 0), kv_col)),
            pl.BlockSpec((blk, kvw), lambda b, n: (0, 0)),
        ] + side_in,
        out_specs=[pl.BlockSpec((blk, d), lambda b, n: (b * nb + n, 0))] + side_out,
        out_shape=[jax.ShapeDtypeStruct((t, d), BF16)] + side_shapes,
        compiler_params=_params(est),
        name="swa_attention",
    )(sinks, qkv, qkv, qkv, kv_meta, *(s.w for s in side_casts))


def _out_proj_kernel(a_ref, w_ref, h_ref, o_ref):
    o_ref[...] = h_ref[...] + _dot(a_ref[...], w_ref[...])


def _out_proj(a, w, h, *, tm):
    t, d = h.shape
    k = a.shape[1]
    est = 2 * tm * k * 2 + 2 * k * d * 2 + 2 * 2 * tm * d * 4 + tm * d * 4
    return pl.pallas_call(
        _out_proj_kernel,
        grid=(t // tm, 1),
        in_specs=[
            pl.BlockSpec((tm, k), lambda i, j: (i, 0)),
            pl.BlockSpec((k, d), lambda i, j: (0, 0)),
            pl.BlockSpec((tm, d), lambda i, j: (i, 0)),
        ],
        out_specs=pl.BlockSpec((tm, d), lambda i, j: (i, 0)),
        out_shape=jax.ShapeDtypeStruct((t, d), F32),
        compiler_params=_params(est),
        name="attn_out_proj",
    )(a, w, h)


def _rope_tables(first_pos, n_pos):
    pos = jnp.arange(first_pos, first_pos + n_pos, dtype=F32)
    inv = ROPE_THETA ** (-jnp.arange(0, HEAD_DIM, 2, dtype=F32) / HEAD_DIM)
    ang = pos[:, None] * inv[None, :]
    cos, sin = jnp.cos(ang), jnp.sin(ang)
    return (jnp.concatenate([cos, cos, cos, cos], axis=1),
            jnp.concatenate([-sin, sin, -sin, sin], axis=1))


def _extend_qkv_block(blk, *, d):
    low = lax.broadcasted_iota(jnp.int32, (1, V7X_LANES), 1) < HEAD_DIM
    pieces = [(blk[:, :d] * HEAD_DIM ** -0.5).astype(BF16)]
    for c0 in range(d, blk.shape[1], V7X_LANES):
        t = blk[:, c0:c0 + V7X_LANES]
        swapped = pltpu.roll(t, HEAD_DIM, axis=1)
        pieces += [jnp.where(low, t, swapped).astype(BF16), jnp.where(low, swapped, t).astype(BF16)]
    return jnp.concatenate(pieces, axis=1)


def kernel(x, meta_tokens, norm_mix_0, w_in_conv, conv_w, w_out_conv, norm_mlp_0, w_up_0, w_down_0,
           norm_mix_1, w_qkv, attn_sinks, w_o, norm_mlp_1, w_up_1, w_down_1, norm_final):
    batch, seq, d = x.shape
    n_meta = meta_tokens.shape[0]
    n_kv = (w_qkv.shape[1] - d) // (2 * HEAD_DIM)
    pad = (-(seq + n_meta)) % ATTN_BLOCK
    assert seq % ATTN_BLOCK == 0 and pad + n_meta == ATTN_BLOCK
    assert n_meta % (2 * V7X_SUBLANES) == 0 and d // HEAD_DIM == 8 * n_kv

    tm, cc, nc, fc = TOKEN_TILE, CONV_CHUNK, COL_CHUNK, FF_CHUNK
    assert seq % tm == 0 and d % nc == 0
    tiles_per_seq = seq // tm
    n_q_chunks = d // nc

    row = lambda g: g.reshape(1, d)
    g_mix0, g_mlp0, g_mix1, g_mlp1, g_fin = map(row, (norm_mix_0, norm_mlp_0, norm_mix_1, norm_mlp_1, norm_final))
    kw = n_kv * HEAD_DIM
    assert 2 * kw == nc, "duplicated k heads (and v heads) must each fill exactly one column chunk"
    cos_m, sin_m = _rope_tables(0, n_meta)
    cos_t, sin_t = _rope_tables(n_meta, seq)
    w_in, w_out = w_in_conv.astype(BF16), w_out_conv.astype(BF16)

    zeros_tail = jnp.zeros((V7X_SUBLANES, d), F32)
    hm, tail_m = _conv_mixer(meta_tokens, g_mix0, w_in, conv_w, w_out, zeros_tail,
                             tm=n_meta, nc=cc, tiles_per_seq=1)
    h = x.reshape(batch * seq, d)
    assert seq % CONV_TILE == 0
    qkv_cast = _SideCast(w_qkv, d + 4 * kw, functools.partial(_extend_qkv_block, d=d))
    h, _, wu0, wd0, w_ext = _conv_mixer(
        h, g_mix0, w_in, conv_w, w_out, tail_m[-V7X_SUBLANES:],
        tm=CONV_TILE, nc=cc, tiles_per_seq=seq // CONV_TILE,
        side_casts=(_plain_cast(w_up_0), _plain_cast(w_down_0), qkv_cast))

    (hm,) = _mlp(hm, g_mlp0, wu0, wd0, g_fin, tm=n_meta, fc=fc, final_norm=False)
    (h,) = _mlp(h, g_mlp0, wu0, wd0, g_fin, tm=tm, fc=fc, final_norm=False)

    (qkv_m,) = _qkv(hm, g_mix1, w_ext, cos_m, sin_m, tm=n_meta, nc=nc, tiles_per_seq=1)
    kv_meta = jnp.pad(qkv_m[:, d:], ((pad, 0), (0, 0)))
    (qkv,) = _qkv(h, g_mix1, w_ext, cos_t, sin_t, tm=tm, nc=nc, tiles_per_seq=tiles_per_seq)
    a, wo, wu1, wd1 = _attention(attn_sinks, qkv, kv_meta, d=d, batch=batch, n_kv=n_kv, pad=pad,
                                 side_casts=(_plain_cast(w_o), _plain_cast(w_up_1), _plain_cast(w_down_1)))
    h = _out_proj(a, wo, h, tm=OUT_PROJ_TILE)
    (out,) = _mlp(h, g_mlp1, wu1, wd1, g_fin, tm=tm, fc=fc, final_norm=True)
    return out.reshape(batch, seq, d)
```

```python
import functools
from typing import Callable, NamedTuple

import jax
import jax.numpy as jnp
from jax import lax
from jax.experimental import pallas as pl
from jax.experimental.pallas import tpu as pltpu

HEAD_DIM = 64
HALF_DIM = HEAD_DIM // 2
ATTN_BLOCK = 128
ROPE_THETA = 10000.0
RMS_EPS = 1e-5
NEG_INF = -1e30
CONV_WIDTH = 3

V7X_LANES = 128
V7X_SUBLANES = 8
V7X_BF16_ROWS = 16
V7X_VMEM_LIMIT_BYTES = 60000 * 1024

TOKEN_TILE = 1024
CONV_TILE = 1024
OUT_PROJ_TILE = 512
CONV_CHUNK = 512
COL_CHUNK = 512
FF_CHUNK = 1024
QKV_ROW_SPLITS = 4
CONV_NORM_ROW_SPLITS = 2
CONV_ROW_SPLITS = 1
NORM_ROW_SPLITS = 8

BF16 = jnp.bfloat16
F32 = jnp.float32


def _dot(a, b):
    return jnp.dot(a, b, preferred_element_type=F32)


def _rms_norm(x, g):
    var = jnp.mean(x * x, axis=-1, keepdims=True)
    return (x * lax.rsqrt(var + RMS_EPS)) * g


def _row_splits(tm, splits):
    return splits if tm % (splits * 128) == 0 else 1


def _params(est_bytes):
    del est_bytes
    return pltpu.CompilerParams(
        dimension_semantics=("arbitrary", "arbitrary"), vmem_limit_bytes=V7X_VMEM_LIMIT_BYTES)


class _SideCast(NamedTuple):
    w: jax.Array
    out_cols: int
    fn: Callable


def _plain_cast(w):
    return _SideCast(w, w.shape[1], lambda blk: blk.astype(BF16))


def _side_cast_specs(sides, nt, nj):
    n_steps = nt * nj
    in_specs, out_specs, out_shapes, vmem = [], [], [], 0
    for s in sides:
        rows, cols = s.w.shape
        n_blocks = 1
        while 2 * n_blocks <= min(n_steps, rows // V7X_BF16_ROWS) and rows % (2 * n_blocks) == 0:
            n_blocks *= 2
        rb = rows // n_blocks
        assert rb % V7X_BF16_ROWS == 0
        index = lambda i, j, nb=n_blocks: (((i * nj + j) * nb) // n_steps, 0)
        in_specs.append(pl.BlockSpec((rb, cols), index))
        out_specs.append(pl.BlockSpec((rb, s.out_cols), index))
        out_shapes.append(jax.ShapeDtypeStruct((rows, s.out_cols), BF16))
        vmem += 2 * rb * (cols * 4 + s.out_cols * 2)
    return in_specs, out_specs, out_shapes, vmem


def _run_side_casts(fns, in_refs, out_refs):
    for fn, w_ref, o_ref in zip(fns, in_refs, out_refs):
        o_ref[...] = fn(w_ref[...])


class _TileStream:
    def __init__(self, x_hbm, o_hbm, buf, in_sem, out_sem):
        self.x_hbm, self.o_hbm, self.buf, self.in_sem, self.out_sem = x_hbm, o_hbm, buf, in_sem, out_sem
        self.i, self.j = pl.program_id(0), pl.program_id(1)
        self.nt, self.nj = pl.num_programs(0), pl.num_programs(1)
        self.tm = buf.shape[1]
        self.slot = self.i % 2
        self.tile = buf.at[self.slot]

    def _rows(self, tile):
        return pl.ds(pl.multiple_of(tile * self.tm, self.tm), self.tm)

    def _arrival(self, tile, slot):
        return pltpu.make_async_copy(self.x_hbm.at[self._rows(tile), :], self.buf.at[slot], self.in_sem.at[slot])

    def _departure(self, tile, slot):
        return pltpu.make_async_copy(self.buf.at[slot], self.o_hbm.at[self._rows(tile), :], self.out_sem.at[slot])

    def begin_step(self, n_steps):
        i, j, other = self.i, self.j, 1 - self.slot

        @pl.when(j == 0)
        def _():
            @pl.when(i == 0)
            def _():
                self._arrival(0, 0).start()
            self._arrival(i, self.slot).wait()

        @pl.when(j == min(1, n_steps - 1))
        def _():
            @pl.when(i > 0)
            def _():
                self._departure(i - 1, other).wait()

            @pl.when(i + 1 < self.nt)
            def _():
                self._arrival(i + 1, other).start()

    def end_step(self, n_steps):
        i = self.i

        @pl.when(self.j == n_steps - 1)
        def _():
            self._departure(i, self.slot).start()

            @pl.when(i == self.nt - 1)
            def _():
                self._departure(i, self.slot).wait()


def _conv_mixer_kernel(*refs, tiles_per_seq, n_steps, side_fns):
    n = len(side_fns)
    x_hbm, g_ref, wb_ref, wc_ref, wu_ref, cw_ref, wo_ref, tail_in_ref = refs[:8]
    o_hbm, tail_out_ref = refs[8 + n:10 + n]
    hn_ref, carry_ref, buf, in_sem, out_sem = refs[10 + 2 * n:]
    stream = _TileStream(x_hbm, o_hbm, buf, in_sem, out_sem)
    stream.begin_step(n_steps)
    o_ref = stream.tile
    i = pl.program_id(0)
    j = pl.program_id(1)
    tm = hn_ref.shape[0]
    cw = cw_ref[...]
    first_tile_of_seq = (i % tiles_per_seq) == 0
    tile_prev = jnp.where(first_tile_of_seq, tail_in_ref[...], carry_ref[j])

    def mix(hn, prev):
        gate_b = _dot(hn, wb_ref[...])
        v = _dot(hn, wc_ref[...]) * _dot(hn, wu_ref[...])
        vcat = jnp.concatenate([prev, v], axis=0)
        v_m1 = pltpu.roll(vcat, 1, axis=0)[V7X_SUBLANES:]
        v_m2 = pltpu.roll(vcat, 2, axis=0)[V7X_SUBLANES:]
        conv = cw[0:1] * v_m2 + cw[1:2] * v_m1 + cw[2:3] * v
        gated = (gate_b * conv).astype(BF16)
        return _dot(gated, wo_ref[...]), v[v.shape[0] - V7X_SUBLANES:]

    def finish(tail):
        carry_ref[j] = tail
        tail_out_ref[...] = tail
        _run_side_casts(side_fns, refs[8:8 + n], refs[10 + n:10 + 2 * n])

    @pl.when(j == 0)
    def _():
        rm = tm // _row_splits(tm, CONV_NORM_ROW_SPLITS)
        prev = tile_prev
        for r0 in range(0, tm, rm):
            rows = slice(r0, r0 + rm)
            x = o_ref[rows, :]
            hn = _rms_norm(x, g_ref[...]).astype(BF16)
            hn_ref[rows, :] = hn
            y, prev = mix(hn, prev)
            o_ref[rows, :] = x + y
        finish(prev)

    @pl.when(j > 0)
    def _():
        rm = tm // _row_splits(tm, CONV_ROW_SPLITS)
        prev = tile_prev
        for r0 in range(0, tm, rm):
            rows = slice(r0, r0 + rm)
            y, prev = mix(hn_ref[rows, :], prev)
            o_ref[rows, :] += y
        finish(prev)

    stream.end_step(n_steps)


def _conv_mixer(x, gain, w_in, conv_w, w_out, tail_in, *, tm, nc, tiles_per_seq, side_casts=()):
    t, d = x.shape
    nt, nj = t // tm, d // nc
    side_in, side_out, side_shapes, side_vmem = _side_cast_specs(side_casts, nt, nj)
    est = (2 * tm * d * 4 + tm * d * 2 + 2 * 4 * d * nc * 2 + 8 * tm * nc * 4) + side_vmem
    kern = functools.partial(_conv_mixer_kernel, tiles_per_seq=tiles_per_seq, n_steps=nj,
                             side_fns=tuple(s.fn for s in side_casts))
    return pl.pallas_call(
        kern,
        grid=(nt, nj),
        in_specs=[
            pl.BlockSpec(memory_space=pl.ANY),
            pl.BlockSpec((1, d), lambda i, j: (0, 0)),
            pl.BlockSpec((d, nc), lambda i, j: (0, j)),
            pl.BlockSpec((d, nc), lambda i, j: (0, nj + j)),
            pl.BlockSpec((d, nc), lambda i, j: (0, 2 * nj + j)),
            pl.BlockSpec((CONV_WIDTH, nc), lambda i, j: (0, j)),
            pl.BlockSpec((nc, d), lambda i, j: (j, 0)),
            pl.BlockSpec((V7X_SUBLANES, nc), lambda i, j: (0, j)),
        ] + side_in,
        out_specs=[
            pl.BlockSpec(memory_space=pl.ANY),
            pl.BlockSpec((V7X_SUBLANES, nc), lambda i, j: (i, j)),
        ] + side_out,
        out_shape=[
            jax.ShapeDtypeStruct((t, d), F32),
            jax.ShapeDtypeStruct((nt * V7X_SUBLANES, d), F32),
        ] + side_shapes,
        scratch_shapes=[
            pltpu.VMEM((tm, d), BF16),
            pltpu.VMEM((nj, V7X_SUBLANES, nc), F32),
            pltpu.VMEM((2, tm, d), F32),
            pltpu.SemaphoreType.DMA((2,)),
            pltpu.SemaphoreType.DMA((2,)),
        ],
        compiler_params=_params(est),
        name="conv_mixer",
    )(x, gain, w_in, w_in, w_in, conv_w, w_out, tail_in, *(s.w for s in side_casts))


def _mlp_kernel(*refs, final_norm, side_fns):
    n = len(side_fns)
    x_ref, g_ref, wu_ref, wd_ref, gf_ref = refs[:5]
    o_ref = refs[5 + n]
    hn_ref = refs[6 + 2 * n]
    j = pl.program_id(1)
    tm = hn_ref.shape[0]
    rm = tm // _row_splits(tm, NORM_ROW_SPLITS)

    def ffn(hn):
        a = jnp.maximum(_dot(hn, wu_ref[...]), 0.0)
        return _dot((a * a).astype(BF16), wd_ref[...])

    side_casts = functools.partial(_run_side_casts, side_fns, refs[5:5 + n], refs[6 + n:6 + 2 * n])

    @pl.when(j == 0)
    def _():
        for r0 in range(0, tm, rm):
            rows = slice(r0, r0 + rm)
            x = x_ref[rows, :]
            hn = _rms_norm(x, g_ref[...]).astype(BF16)
            hn_ref[rows, :] = hn
            o_ref[rows, :] = x + ffn(hn)
        side_casts()

    @pl.when(j > 0)
    def _():
        o_ref[...] += ffn(hn_ref[...])
        side_casts()

    if final_norm:
        @pl.when(j == pl.num_programs(1) - 1)
        def _():
            o_ref[...] = _rms_norm(o_ref[...], gf_ref[...])


def _mlp(x, gain, w_up, w_down, gain_final, *, tm, fc, final_norm, side_casts=()):
    t, d = x.shape
    f = w_up.shape[1]
    nt, nj = t // tm, f // fc
    side_in, side_out, side_shapes, side_vmem = _side_cast_specs(side_casts, nt, nj)
    est = 2 * 2 * tm * d * 4 + tm * d * 2 + 2 * 2 * d * fc * 2 + 3 * tm * fc * 4 + side_vmem
    kern = functools.partial(_mlp_kernel, final_norm=final_norm, side_fns=tuple(s.fn for s in side_casts))
    return pl.pallas_call(
        kern,
        grid=(nt, nj),
        in_specs=[
            pl.BlockSpec((tm, d), lambda i, j: (i, 0)),
            pl.BlockSpec((1, d), lambda i, j: (0, 0)),
            pl.BlockSpec((d, fc), lambda i, j: (0, j)),
            pl.BlockSpec((fc, d), lambda i, j: (j, 0)),
            pl.BlockSpec((1, d), lambda i, j: (0, 0)),
        ] + side_in,
        out_specs=[pl.BlockSpec((tm, d), lambda i, j: (i, 0))] + side_out,
        out_shape=[jax.ShapeDtypeStruct((t, d), F32)] + side_shapes,
        scratch_shapes=[pltpu.VMEM((tm, d), BF16)],
        compiler_params=_params(est),
        name="mlp_final" if final_norm else "mlp",
    )(x, gain, w_up, w_down, gain_final, *(s.w for s in side_casts))


def _qkv_kernel(*refs, v_chunk, side_fns):
    n = len(side_fns)
    x_ref, g_ref, w_ref, cos_ref, sin_ref = refs[:5]
    o_ref = refs[5 + n]
    hn_ref = refs[6 + 2 * n]
    side_casts = functools.partial(_run_side_casts, side_fns, refs[5:5 + n], refs[6 + n:6 + 2 * n])
    j = pl.program_id(1)
    is_v = j == v_chunk
    lane = lax.broadcasted_iota(jnp.int32, (1, V7X_LANES), 1)
    first_half = (lane % HEAD_DIM) < HALF_DIM
    tm = hn_ref.shape[0]
    rm = tm // _row_splits(tm, QKV_ROW_SPLITS)

    def project(rows, hn):
        y = _dot(hn, w_ref[...])
        cos = jnp.where(is_v, 1.0, cos_ref[rows, :])
        sin = jnp.where(is_v, 0.0, sin_ref[rows, :])
        for c in range(y.shape[1] // V7X_LANES):
            sl = slice(c * V7X_LANES, (c + 1) * V7X_LANES)
            t = y[:, sl]
            rot = jnp.where(first_half,
                            pltpu.roll(t, V7X_LANES - HALF_DIM, axis=1),
                            pltpu.roll(t, HALF_DIM, axis=1))
            o_ref[rows, sl] = (t * cos + rot * sin).astype(BF16)

    @pl.when(j == 0)
    def _():
        for r0 in range(0, tm, rm):
            rows = slice(r0, r0 + rm)
            hn = _rms_norm(x_ref[rows, :], g_ref[...]).astype(BF16)
            hn_ref[rows, :] = hn
            project(rows, hn)
        side_casts()

    @pl.when(j > 0)
    def _():
        for r0 in range(0, tm, rm):
            rows = slice(r0, r0 + rm)
            project(rows, hn_ref[rows, :])
        side_casts()


def _qkv(x, gain, w_ext, cos_t, sin_t, *, tm, nc, tiles_per_seq, side_casts=()):
    t, d = x.shape
    nw = w_ext.shape[1]
    nt, nj = t // tm, nw // nc
    side_in, side_out, side_shapes, side_vmem = _side_cast_specs(side_casts, nt, nj)
    est = (2 * tm * d * 4 + tm * d * 2 + 2 * d * nc * 2 + 2 * 2 * tm * 128 * 4 + 2 * tm * nc * 2
           + 4 * tm * nc * 4 + side_vmem)
    kern = functools.partial(_qkv_kernel, v_chunk=nj - 1, side_fns=tuple(s.fn for s in side_casts))
    return pl.pallas_call(
        kern,
        grid=(nt, nj),
        in_specs=[
            pl.BlockSpec((tm, d), lambda i, j: (i, 0)),
            pl.BlockSpec((1, d), lambda i, j: (0, 0)),
            pl.BlockSpec((d, nc), lambda i, j: (0, j)),
            pl.BlockSpec((tm, V7X_LANES), lambda i, j: (i % tiles_per_seq, 0)),
            pl.BlockSpec((tm, V7X_LANES), lambda i, j: (i % tiles_per_seq, 0)),
        ] + side_in,
        out_specs=[pl.BlockSpec((tm, nc), lambda i, j: (i, j))] + side_out,
        out_shape=[jax.ShapeDtypeStruct((t, nw), BF16)] + side_shapes,
        scratch_shapes=[pltpu.VMEM((tm, d), BF16)],
        compiler_params=_params(est),
        name="qkv_rope",
    )(x, gain, w_ext, cos_t, sin_t, *(s.w for s in side_casts))


def _attn_kernel(*refs, n_kv, pad, side_fns):
    ns = len(side_fns)
    sink_ref, q_ref, kvc_ref, kvp_ref, kvm_ref = refs[:5]
    o_ref = refs[5 + ns]
    _run_side_casts(side_fns, refs[5:5 + ns], refs[6 + ns:6 + 2 * ns])
    n = pl.program_id(1)
    blk = ATTN_BLOCK
    kv_prev = jnp.where(n == 0, kvm_ref[...], kvp_ref[...])
    band = jnp.concatenate([kv_prev, kvc_ref[...]], axis=0)

    qi = lax.broadcasted_iota(jnp.int32, (blk, 2 * blk), 0)
    kj = lax.broadcasted_iota(jnp.int32, (blk, 2 * blk), 1)
    allowed = (kj > qi) & (kj <= qi + blk) & (kj + n * blk >= pad)

    sink_slot = lax.broadcasted_iota(jnp.int32, (1, 2 * blk), 1) == 0
    low = lax.broadcasted_iota(jnp.int32, (2 * blk, V7X_LANES), 1) < HEAD_DIM
    not_sink_row = lax.broadcasted_iota(jnp.int32, (V7X_BF16_ROWS, V7X_LANES), 0) > 0
    zero = jnp.zeros((), BF16)
    low_row = lax.broadcasted_iota(jnp.int32, (1, V7X_LANES), 1) < HEAD_DIM
    ones_low = jnp.broadcast_to(jnp.where(low_row, 1.0, 0.0).astype(BF16), (2 * blk, V7X_LANES))
    ones_high = jnp.broadcast_to(jnp.where(low_row, 0.0, 1.0).astype(BF16), (2 * blk, V7X_LANES))
    pairs = 4
    contract_lanes = (((1,), (1,)), ((), ()))

    for h in range(n_kv):
        kd = band[:, h * V7X_LANES:(h + 1) * V7X_LANES]
        vd = band[:, (n_kv + h) * V7X_LANES:(n_kv + h + 1) * V7X_LANES]
        sink_tile = jnp.where(not_sink_row, vd[:V7X_BF16_ROWS].astype(F32), 0.0).astype(BF16)
        vd = jnp.concatenate([sink_tile, vd[V7X_BF16_ROWS:]], axis=0)
        k_sides = (jnp.where(low, kd, zero), jnp.where(low, zero, kd))
        values = jnp.concatenate([
            jnp.concatenate([jnp.where(low, vd, zero), ones_low], axis=1),
            jnp.concatenate([jnp.where(low, zero, vd), ones_high], axis=1)], axis=0)
        qbase = h * pairs * V7X_LANES
        qst = jnp.concatenate(
            [q_ref[:, qbase + p * V7X_LANES: qbase + (p + 1) * V7X_LANES] for p in range(pairs)],
            axis=0)
        e_sides = []
        for side in range(2):
            s_all = lax.dot_general(qst, k_sides[side], contract_lanes, preferred_element_type=F32)
            es = []
            for p in range(pairs):
                fill = jnp.where(sink_slot, sink_ref[h * 2 * pairs + 2 * p + side], NEG_INF)
                s = jnp.where(allowed, s_all[p * blk:(p + 1) * blk], fill)
                e = jnp.exp(s - jnp.max(s, axis=-1, keepdims=True))
                es.append(e.astype(BF16))
            e_sides.append(jnp.concatenate(es, axis=0))
        o_den = _dot(jnp.concatenate(e_sides, axis=1), values)
        out = o_den[:, :V7X_LANES] * (1.0 / o_den[:, V7X_LANES:])
        for p in range(pairs):
            o_ref[:, qbase + p * V7X_LANES: qbase + (p + 1) * V7X_LANES] = (
                out[p * blk:(p + 1) * blk].astype(BF16))


def _attention(sinks, qkv, kv_meta, *, d, batch, n_kv, pad, side_casts=()):
    t = qkv.shape[0]
    blk = ATTN_BLOCK
    nb = t // batch // blk
    kvw = kv_meta.shape[1]
    kv_col = d // kvw
    assert kv_col * kvw == d and qkv.shape[1] == d + kvw
    side_in, side_out, side_shapes, side_vmem = _side_cast_specs(side_casts, batch, nb)
    est = 2 * 2 * blk * d * 2 + 3 * 2 * blk * kvw * 2 + 24 * 512 * 256 * 4 + side_vmem
    kern = functools.partial(_attn_kernel, n_kv=n_kv, pad=pad, side_fns=tuple(s.fn for s in side_casts))
    return pl.pallas_call(
        kern,
        grid=(batch, nb),
        in_specs=[
            pl.BlockSpec(memory_space=pltpu.SMEM),
            pl.BlockSpec((blk, d), lambda b, n: (b * nb + n, 0)),
            pl.BlockSpec((blk, kvw), lambda b, n: (b * nb + n, kv_col)),
            pl.BlockSpec((blk, kvw), lambda b, n: (b * nb + jnp.maximum(n - 1, 0), kv_col)),
            pl.BlockSpec((blk, kvw), lambda b, n: (0, 0)),
        ] + side_in,
        out_specs=[pl.BlockSpec((blk, d), lambda b, n: (b * nb + n, 0))] + side_out,
        out_shape=[jax.ShapeDtypeStruct((t, d), BF16)] + side_shapes,
        compiler_params=_params(est),
        name="swa_attention",
    )(sinks, qkv, qkv, qkv, kv_meta, *(s.w for s in side_casts))


def _out_proj_kernel(a_ref, w_ref, h_ref, o_ref):
    o_ref[...] = h_ref[...] + _dot(a_ref[...], w_ref[...])


def _out_proj(a, w, h, *, tm):
    t, d = h.shape
    k = a.shape[1]
    est = 2 * tm * k * 2 + 2 * k * d * 2 + 2 * 2 * tm * d * 4 + tm * d * 4
    return pl.pallas_call(
        _out_proj_kernel,
        grid=(t // tm, 1),
        in_specs=[
            pl.BlockSpec((tm, k), lambda i, j: (i, 0)),
            pl.BlockSpec((k, d), lambda i, j: (0, 0)),
            pl.BlockSpec((tm, d), lambda i, j: (i, 0)),
        ],
        out_specs=pl.BlockSpec((tm, d), lambda i, j: (i, 0)),
        out_shape=jax.ShapeDtypeStruct((t, d), F32),
        compiler_params=_params(est),
        name="attn_out_proj",
    )(a, w, h)


def _rope_tables(first_pos, n_pos):
    pos = jnp.arange(first_pos, first_pos + n_pos, dtype=F32)
    inv = ROPE_THETA ** (-jnp.arange(0, HEAD_DIM, 2, dtype=F32) / HEAD_DIM)
    inv = jnp.concatenate([inv, inv, inv, inv])
    sign = jnp.concatenate([-jnp.ones_like(inv[:HALF_DIM]), jnp.ones_like(inv[:HALF_DIM])] * 2)
    ang = pos[:, None] * inv[None, :]
    return jnp.cos(ang), jnp.sin(ang) * sign


def _extend_qkv_block(blk, *, d):
    low = lax.broadcasted_iota(jnp.int32, (1, V7X_LANES), 1) < HEAD_DIM
    pieces = [(blk[:, :d] * HEAD_DIM ** -0.5).astype(BF16)]
    for c0 in range(d, blk.shape[1], V7X_LANES):
        t = blk[:, c0:c0 + V7X_LANES]
        swapped = pltpu.roll(t, HEAD_DIM, axis=1)
        pieces += [jnp.where(low, t, swapped).astype(BF16), jnp.where(low, swapped, t).astype(BF16)]
    return jnp.concatenate(pieces, axis=1)


def kernel(x, meta_tokens, norm_mix_0, w_in_conv, conv_w, w_out_conv, norm_mlp_0, w_up_0, w_down_0,
           norm_mix_1, w_qkv, attn_sinks, w_o, norm_mlp_1, w_up_1, w_down_1, norm_final):
    batch, seq, d = x.shape
    n_meta = meta_tokens.shape[0]
    n_kv = (w_qkv.shape[1] - d) // (2 * HEAD_DIM)
    pad = (-(seq + n_meta)) % ATTN_BLOCK
    assert seq % ATTN_BLOCK == 0 and pad + n_meta == ATTN_BLOCK
    assert n_meta % (2 * V7X_SUBLANES) == 0 and d // HEAD_DIM == 8 * n_kv

    tm, cc, nc, fc = TOKEN_TILE, CONV_CHUNK, COL_CHUNK, FF_CHUNK
    assert seq % tm == 0 and d % nc == 0
    tiles_per_seq = seq // tm
    n_q_chunks = d // nc

    row = lambda g: g.reshape(1, d)
    g_mix0, g_mlp0, g_mix1, g_mlp1, g_fin = map(row, (norm_mix_0, norm_mlp_0, norm_mix_1, norm_mlp_1, norm_final))
    kw = n_kv * HEAD_DIM
    assert 2 * kw == nc, "duplicated k heads (and v heads) must each fill exactly one column chunk"
    cos_m, sin_m = _rope_tables(0, n_meta)
    cos_t, sin_t = _rope_tables(n_meta, seq)
    w_in, w_out = w_in_conv.astype(BF16), w_out_conv.astype(BF16)

    zeros_tail = jnp.zeros((V7X_SUBLANES, d), F32)
    hm, tail_m = _conv_mixer(meta_tokens, g_mix0, w_in, conv_w, w_out, zeros_tail,
                             tm=n_meta, nc=cc, tiles_per_seq=1)
    h = x.reshape(batch * seq, d)
    assert seq % CONV_TILE == 0
    qkv_cast = _SideCast(w_qkv, d + 4 * kw, functools.partial(_extend_qkv_block, d=d))
    h, _, wu0, wd0, w_ext, wo, wu1, wd1 = _conv_mixer(
        h, g_mix0, w_in, conv_w, w_out, tail_m[-V7X_SUBLANES:],
        tm=CONV_TILE, nc=cc, tiles_per_seq=seq // CONV_TILE,
        side_casts=(_plain_cast(w_up_0), _plain_cast(w_down_0), qkv_cast, _plain_cast(w_o),
                    _plain_cast(w_up_1), _plain_cast(w_down_1)))

    (hm,) = _mlp(hm, g_mlp0, wu0, wd0, g_fin, tm=n_meta, fc=fc, final_norm=False)
    (h,) = _mlp(h, g_mlp0, wu0, wd0, g_fin, tm=tm, fc=fc, final_norm=False)

    (qkv_m,) = _qkv(hm, g_mix1, w_ext, cos_m, sin_m, tm=n_meta, nc=nc, tiles_per_seq=1)
    kv_meta = jnp.pad(qkv_m[:, d:], ((pad, 0), (0, 0)))
    (qkv,) = _qkv(h, g_mix1, w_ext, cos_t, sin_t, tm=tm, nc=nc, tiles_per_seq=tiles_per_seq)
    (a,) = _attention(attn_sinks, qkv, kv_meta, d=d, batch=batch, n_kv=n_kv, pad=pad)
    h = _out_proj(a, wo, h, tm=OUT_PROJ_TILE)
    (out,) = _mlp(h, g_mlp1, wu1, wd1, g_fin, tm=tm, fc=fc, final_norm=True)
    return out.reshape(batch, seq, d)
```

```python
import functools
from typing import Callable, NamedTuple

import jax
import jax.numpy as jnp
from jax import lax
from jax.experimental import pallas as pl
from jax.experimental.pallas import tpu as pltpu

HEAD_DIM = 64
HALF_DIM = HEAD_DIM // 2
ATTN_BLOCK = 128
ROPE_THETA = 10000.0
RMS_EPS = 1e-5
NEG_INF = -1e30
CONV_WIDTH = 3

V7X_LANES = 128
V7X_SUBLANES = 8
V7X_BF16_ROWS = 16
V7X_VMEM_LIMIT_BYTES = 60000 * 1024

TOKEN_TILE = 1024
CONV_TILE = 1024
OUT_PROJ_TILE = 512
CONV_CHUNK = 512
META_CONV_CHUNK = 256
COL_CHUNK = 512
FF_CHUNK = 1024
QKV_ROW_SPLITS = 8
CONV_NORM_ROW_SPLITS = 2
CONV_ROW_SPLITS = 1
NORM_ROW_SPLITS = 8

BF16 = jnp.bfloat16
F32 = jnp.float32


def _dot(a, b):
    return jnp.dot(a, b, preferred_element_type=F32)


def _rms_norm(x, g):
    var = jnp.mean(x * x, axis=-1, keepdims=True)
    return (x * lax.rsqrt(var + RMS_EPS)) * g


def _row_splits(tm, splits):
    return splits if tm % (splits * 128) == 0 else 1


def _params(est_bytes):
    del est_bytes
    return pltpu.CompilerParams(
        dimension_semantics=("arbitrary", "arbitrary"), vmem_limit_bytes=V7X_VMEM_LIMIT_BYTES)


class _SideCast(NamedTuple):
    w: jax.Array
    out_cols: int
    fn: Callable


def _plain_cast(w):
    return _SideCast(w, w.shape[1], lambda blk: blk.astype(BF16))


def _side_cast_specs(sides, nt, nj):
    n_steps = nt * nj
    in_specs, out_specs, out_shapes, vmem = [], [], [], 0
    for s in sides:
        rows, cols = s.w.shape
        n_blocks = 1
        while 2 * n_blocks <= min(n_steps, rows // V7X_BF16_ROWS) and rows % (2 * n_blocks) == 0:
            n_blocks *= 2
        rb = rows // n_blocks
        assert rb % V7X_BF16_ROWS == 0
        index = lambda i, j, nb=n_blocks: (((i * nj + j) * nb) // n_steps, 0)
        in_specs.append(pl.BlockSpec((rb, cols), index))
        out_specs.append(pl.BlockSpec((rb, s.out_cols), index))
        out_shapes.append(jax.ShapeDtypeStruct((rows, s.out_cols), BF16))
        vmem += 2 * rb * (cols * 4 + s.out_cols * 2)
    return in_specs, out_specs, out_shapes, vmem


def _run_side_casts(fns, in_refs, out_refs):
    for fn, w_ref, o_ref in zip(fns, in_refs, out_refs):
        o_ref[...] = fn(w_ref[...])


class _TileStream:
    def __init__(self, x_hbm, o_hbm, buf, in_sem, out_sem):
        self.x_hbm, self.o_hbm, self.buf, self.in_sem, self.out_sem = x_hbm, o_hbm, buf, in_sem, out_sem
        self.i, self.j = pl.program_id(0), pl.program_id(1)
        self.nt, self.nj = pl.num_programs(0), pl.num_programs(1)
        self.tm = buf.shape[1]
        self.slot = self.i % 2
        self.tile = buf.at[self.slot]

    def _rows(self, tile):
        return pl.ds(pl.multiple_of(tile * self.tm, self.tm), self.tm)

    def _arrival(self, tile, slot):
        return pltpu.make_async_copy(self.x_hbm.at[self._rows(tile), :], self.buf.at[slot], self.in_sem.at[slot])

    def _departure(self, tile, slot):
        return pltpu.make_async_copy(self.buf.at[slot], self.o_hbm.at[self._rows(tile), :], self.out_sem.at[slot])

    def begin_step(self, n_steps):
        i, j, other = self.i, self.j, 1 - self.slot

        @pl.when(j == 0)
        def _():
            @pl.when(i == 0)
            def _():
                self._arrival(0, 0).start()
            self._arrival(i, self.slot).wait()

        @pl.when(j == min(1, n_steps - 1))
        def _():
            @pl.when(i > 0)
            def _():
                self._departure(i - 1, other).wait()

            @pl.when(i + 1 < self.nt)
            def _():
                self._arrival(i + 1, other).start()

    def end_step(self, n_steps):
        i = self.i

        @pl.when(self.j == n_steps - 1)
        def _():
            self._departure(i, self.slot).start()

            @pl.when(i == self.nt - 1)
            def _():
                self._departure(i, self.slot).wait()


def _conv_mixer_kernel(*refs, tiles_per_seq, n_steps, side_fns, cast_weights):
    n = len(side_fns)
    n_w = 4 if cast_weights else 0
    x_hbm, g_ref, wb_ref, wc_ref, wu_ref, cw_ref, wo_ref, tail_in_ref = refs[:8]
    o_hbm, tail_out_ref = refs[8 + n:10 + n]
    w_bf16_refs = refs[10 + 2 * n:10 + 2 * n + n_w]
    hn_ref, carry_ref, buf, in_sem, out_sem = refs[10 + 2 * n + n_w:]
    stream = _TileStream(x_hbm, o_hbm, buf, in_sem, out_sem)
    stream.begin_step(n_steps)
    o_ref = stream.tile
    i = pl.program_id(0)
    j = pl.program_id(1)
    tm = hn_ref.shape[0]
    cw = cw_ref[...]
    first_tile_of_seq = (i % tiles_per_seq) == 0
    tile_prev = jnp.where(first_tile_of_seq, tail_in_ref[...], carry_ref[j])

    weights = (wb_ref, wc_ref, wu_ref, wo_ref)
    if cast_weights:
        weights = tuple(w[...].astype(BF16) for w in weights)
        for w_out_ref, w in zip(w_bf16_refs, weights):
            w_out_ref[...] = w
    wb, wc, wu, wo = weights

    def mix(hn, prev):
        gate_b = _dot(hn, wb[...])
        v = _dot(hn, wc[...]) * _dot(hn, wu[...])
        vcat = jnp.concatenate([prev, v], axis=0)
        v_m1 = pltpu.roll(vcat, 1, axis=0)[V7X_SUBLANES:]
        v_m2 = pltpu.roll(vcat, 2, axis=0)[V7X_SUBLANES:]
        conv = cw[0:1] * v_m2 + cw[1:2] * v_m1 + cw[2:3] * v
        gated = (gate_b * conv).astype(BF16)
        return _dot(gated, wo[...]), v[v.shape[0] - V7X_SUBLANES:]

    def finish(tail):
        carry_ref[j] = tail
        tail_out_ref[...] = tail
        _run_side_casts(side_fns, refs[8:8 + n], refs[10 + n:10 + 2 * n])

    @pl.when(j == 0)
    def _():
        rm = tm // _row_splits(tm, CONV_NORM_ROW_SPLITS)
        prev = tile_prev
        for r0 in range(0, tm, rm):
            rows = slice(r0, r0 + rm)
            x = o_ref[rows, :]
            hn = _rms_norm(x, g_ref[...]).astype(BF16)
            hn_ref[rows, :] = hn
            y, prev = mix(hn, prev)
            o_ref[rows, :] = x + y
        finish(prev)

    @pl.when(j > 0)
    def _():
        rm = tm // _row_splits(tm, CONV_ROW_SPLITS)
        prev = tile_prev
        for r0 in range(0, tm, rm):
            rows = slice(r0, r0 + rm)
            y, prev = mix(hn_ref[rows, :], prev)
            o_ref[rows, :] += y
        finish(prev)

    stream.end_step(n_steps)


def _conv_mixer(x, gain, w_in_parts, conv_w, w_out, tail_in, *, tm, nc, tiles_per_seq, side_casts=(),
                cast_weights=False):
    t, d = x.shape
    nt, nj = t // tm, d // nc
    side_in, side_out, side_shapes, side_vmem = _side_cast_specs(side_casts, nt, nj)
    est = (2 * tm * d * 4 + tm * d * 2 + 2 * 4 * d * nc * 2 + 8 * tm * nc * 4) + side_vmem
    kern = functools.partial(_conv_mixer_kernel, tiles_per_seq=tiles_per_seq, n_steps=nj,
                             side_fns=tuple(s.fn for s in side_casts), cast_weights=cast_weights)
    in_proj_specs = [pl.BlockSpec((d, nc), lambda i, j, c0=col0 // nc: (0, c0 + j)) for _, col0 in w_in_parts]
    w_copy_specs, w_copy_shapes = [], []
    if cast_weights:
        w_copy_specs = [pl.BlockSpec((d, nc), lambda i, j: (0, j))] * 3 + [pl.BlockSpec((nc, d), lambda i, j: (j, 0))]
        w_copy_shapes = [jax.ShapeDtypeStruct((d, d), BF16)] * 4
    return pl.pallas_call(
        kern,
        grid=(nt, nj),
        in_specs=[
            pl.BlockSpec(memory_space=pl.ANY),
            pl.BlockSpec((1, d), lambda i, j: (0, 0)),
            *in_proj_specs,
            pl.BlockSpec((CONV_WIDTH, nc), lambda i, j: (0, j)),
            pl.BlockSpec((nc, d), lambda i, j: (j, 0)),
            pl.BlockSpec((V7X_SUBLANES, nc), lambda i, j: (0, j)),
        ] + side_in,
        out_specs=[
            pl.BlockSpec(memory_space=pl.ANY),
            pl.BlockSpec((V7X_SUBLANES, nc), lambda i, j: (i, j)),
        ] + side_out + w_copy_specs,
        out_shape=[
            jax.ShapeDtypeStruct((t, d), F32),
            jax.ShapeDtypeStruct((nt * V7X_SUBLANES, d), F32),
        ] + side_shapes + w_copy_shapes,
        scratch_shapes=[
            pltpu.VMEM((tm, d), BF16),
            pltpu.VMEM((nj, V7X_SUBLANES, nc), F32),
            pltpu.VMEM((2, tm, d), F32),
            pltpu.SemaphoreType.DMA((2,)),
            pltpu.SemaphoreType.DMA((2,)),
        ],
        compiler_params=_params(est),
        name="conv_mixer",
    )(x, gain, *(w for w, _ in w_in_parts), conv_w, w_out, tail_in, *(s.w for s in side_casts))


def _mlp_kernel(*refs, final_norm, side_fns):
    n = len(side_fns)
    x_ref, g_ref, wu_ref, wd_ref, gf_ref = refs[:5]
    o_ref = refs[5 + n]
    hn_ref = refs[6 + 2 * n]
    j = pl.program_id(1)
    tm = hn_ref.shape[0]
    rm = tm // _row_splits(tm, NORM_ROW_SPLITS)

    def ffn(hn):
        a = jnp.maximum(_dot(hn, wu_ref[...]), 0.0)
        return _dot((a * a).astype(BF16), wd_ref[...])

    side_casts = functools.partial(_run_side_casts, side_fns, refs[5:5 + n], refs[6 + n:6 + 2 * n])

    @pl.when(j == 0)
    def _():
        for r0 in range(0, tm, rm):
            rows = slice(r0, r0 + rm)
            x = x_ref[rows, :]
            hn = _rms_norm(x, g_ref[...]).astype(BF16)
            hn_ref[rows, :] = hn
            o_ref[rows, :] = x + ffn(hn)
        side_casts()

    @pl.when(j > 0)
    def _():
        o_ref[...] += ffn(hn_ref[...])
        side_casts()

    if final_norm:
        @pl.when(j == pl.num_programs(1) - 1)
        def _():
            o_ref[...] = _rms_norm(o_ref[...], gf_ref[...])


def _mlp(x, gain, w_up, w_down, gain_final, *, tm, fc, final_norm, side_casts=()):
    t, d = x.shape
    f = w_up.shape[1]
    nt, nj = t // tm, f // fc
    side_in, side_out, side_shapes, side_vmem = _side_cast_specs(side_casts, nt, nj)
    est = 2 * 2 * tm * d * 4 + tm * d * 2 + 2 * 2 * d * fc * 2 + 3 * tm * fc * 4 + side_vmem
    kern = functools.partial(_mlp_kernel, final_norm=final_norm, side_fns=tuple(s.fn for s in side_casts))
    return pl.pallas_call(
        kern,
        grid=(nt, nj),
        in_specs=[
            pl.BlockSpec((tm, d), lambda i, j: (i, 0)),
            pl.BlockSpec((1, d), lambda i, j: (0, 0)),
            pl.BlockSpec((d, fc), lambda i, j: (0, j)),
            pl.BlockSpec((fc, d), lambda i, j: (j, 0)),
            pl.BlockSpec((1, d), lambda i, j: (0, 0)),
        ] + side_in,
        out_specs=[pl.BlockSpec((tm, d), lambda i, j: (i, 0))] + side_out,
        out_shape=[jax.ShapeDtypeStruct((t, d), F32)] + side_shapes,
        scratch_shapes=[pltpu.VMEM((tm, d), BF16)],
        compiler_params=_params(est),
        name="mlp_final" if final_norm else "mlp",
    )(x, gain, w_up, w_down, gain_final, *(s.w for s in side_casts))


def _qkv_kernel(*refs, kv_chunk, side_fns):
    n = len(side_fns)
    x_ref, g_ref, w_ref, cos_ref, sin_ref = refs[:5]
    o_ref = refs[5 + n]
    hn_ref = refs[6 + 2 * n]
    side_casts = functools.partial(_run_side_casts, side_fns, refs[5:5 + n], refs[6 + n:6 + 2 * n])
    j = pl.program_id(1)
    is_kv = j == kv_chunk
    lane = lax.broadcasted_iota(jnp.int32, (1, V7X_LANES), 1)
    first_half = (lane % HEAD_DIM) < HALF_DIM
    tm = hn_ref.shape[0]
    rm = tm // _row_splits(tm, QKV_ROW_SPLITS)

    def project(rows, hn):
        y = _dot(hn, w_ref[...])
        n_slabs = y.shape[1] // V7X_LANES
        cos, sin = cos_ref[rows, :], sin_ref[rows, :]
        cos_v, sin_v = jnp.where(is_kv, 1.0, cos), jnp.where(is_kv, 0.0, sin)
        for c in range(n_slabs):
            sl = slice(c * V7X_LANES, (c + 1) * V7X_LANES)
            t = y[:, sl]
            rot = jnp.where(first_half,
                            pltpu.roll(t, V7X_LANES - HALF_DIM, axis=1),
                            pltpu.roll(t, HALF_DIM, axis=1))
            in_v_half = c >= n_slabs // 2
            o_ref[rows, sl] = (t * (cos_v if in_v_half else cos) + rot * (sin_v if in_v_half else sin)).astype(BF16)

    @pl.when(j == 0)
    def _():
        for r0 in range(0, tm, rm):
            rows = slice(r0, r0 + rm)
            hn = _rms_norm(x_ref[rows, :], g_ref[...]).astype(BF16)
            hn_ref[rows, :] = hn
            project(rows, hn)
        side_casts()

    @pl.when(j > 0)
    def _():
        for r0 in range(0, tm, rm):
            rows = slice(r0, r0 + rm)
            project(rows, hn_ref[rows, :])
        side_casts()


def _qkv(x, gain, w_ext, cos_t, sin_t, *, tm, nc, tiles_per_seq, side_casts=()):
    t, d = x.shape
    nw = w_ext.shape[1]
    nt, nj = t // tm, nw // nc
    side_in, side_out, side_shapes, side_vmem = _side_cast_specs(side_casts, nt, nj)
    est = (2 * tm * d * 4 + tm * d * 2 + 2 * d * nc * 2 + 2 * 2 * tm * 128 * 4 + 2 * tm * nc * 2
           + 4 * tm * nc * 4 + side_vmem)
    kern = functools.partial(_qkv_kernel, kv_chunk=nj - 1, side_fns=tuple(s.fn for s in side_casts))
    return pl.pallas_call(
        kern,
        grid=(nt, nj),
        in_specs=[
            pl.BlockSpec((tm, d), lambda i, j: (i, 0)),
            pl.BlockSpec((1, d), lambda i, j: (0, 0)),
            pl.BlockSpec((d, nc), lambda i, j: (0, j)),
            pl.BlockSpec((tm, V7X_LANES), lambda i, j: (i % tiles_per_seq, 0)),
            pl.BlockSpec((tm, V7X_LANES), lambda i, j: (i % tiles_per_seq, 0)),
        ] + side_in,
        out_specs=[pl.BlockSpec((tm, nc), lambda i, j: (i, j))] + side_out,
        out_shape=[jax.ShapeDtypeStruct((t, nw), BF16)] + side_shapes,
        scratch_shapes=[pltpu.VMEM((tm, d), BF16)],
        compiler_params=_params(est),
        name="qkv_rope",
    )(x, gain, w_ext, cos_t, sin_t, *(s.w for s in side_casts))


def _attn_kernel(*refs, n_kv, pad, side_fns):
    ns = len(side_fns)
    sink_ref, q_ref, kvc_ref, kvp_ref, kvm_ref = refs[:5]
    o_ref = refs[5 + ns]
    _run_side_casts(side_fns, refs[5:5 + ns], refs[6 + ns:6 + 2 * ns])
    n = pl.program_id(1)
    blk = ATTN_BLOCK
    kv_prev = jnp.where(n == 0, kvm_ref[...], kvp_ref[...])
    band = jnp.concatenate([kv_prev, kvc_ref[...]], axis=0)

    qi = lax.broadcasted_iota(jnp.int32, (blk, 2 * blk), 0)
    kj = lax.broadcasted_iota(jnp.int32, (blk, 2 * blk), 1)
    allowed = (kj > qi) & (kj <= qi + blk) & (kj + n * blk >= pad)

    sink_slot = lax.broadcasted_iota(jnp.int32, (1, 2 * blk), 1) == 0
    low = lax.broadcasted_iota(jnp.int32, (2 * blk, V7X_LANES), 1) < HEAD_DIM
    not_sink_row = lax.broadcasted_iota(jnp.int32, (V7X_BF16_ROWS, V7X_LANES), 0) > 0
    zero = jnp.zeros((), BF16)
    kv_width = n_kv * HEAD_DIM

    def slab_and_swapped(first_col, clear_sink_row):
        slab = band[:, first_col:first_col + V7X_LANES]
        wide = slab.astype(F32)
        if clear_sink_row:
            first = jnp.where(not_sink_row, wide[:V7X_BF16_ROWS], 0.0)
            wide = jnp.concatenate([first, wide[V7X_BF16_ROWS:]], axis=0)
            slab = wide.astype(BF16)
        return slab, pltpu.roll(wide, HEAD_DIM, axis=1).astype(BF16)

    k_slabs = [slab_and_swapped(c0, False) for c0 in range(0, kv_width, V7X_LANES)]
    v_slabs = [slab_and_swapped(kv_width + c0, True) for c0 in range(0, kv_width, V7X_LANES)]

    def head_sides(slabs, h):
        slab, swapped = slabs[h // 2]
        return (slab, swapped) if h % 2 == 0 else (swapped, slab)
    low_row = lax.broadcasted_iota(jnp.int32, (1, V7X_LANES), 1) < HEAD_DIM
    ones_low = jnp.broadcast_to(jnp.where(low_row, 1.0, 0.0).astype(BF16), (2 * blk, V7X_LANES))
    ones_high = jnp.broadcast_to(jnp.where(low_row, 0.0, 1.0).astype(BF16), (2 * blk, V7X_LANES))
    pairs = 4
    contract_lanes = (((1,), (1,)), ((), ()))

    for h in range(n_kv):
        k_low, k_high = head_sides(k_slabs, h)
        v_low, v_high = head_sides(v_slabs, h)
        k_sides = (jnp.where(low, k_low, zero), jnp.where(low, zero, k_high))
        values = jnp.concatenate([
            jnp.concatenate([jnp.where(low, v_low, zero), ones_low], axis=1),
            jnp.concatenate([jnp.where(low, zero, v_high), ones_high], axis=1)], axis=0)
        qbase = h * pairs * V7X_LANES
        qst = jnp.concatenate(
            [q_ref[:, qbase + p * V7X_LANES: qbase + (p + 1) * V7X_LANES] for p in range(pairs)],
            axis=0)
        e_sides = []
        for side in range(2):
            s_all = lax.dot_general(qst, k_sides[side], contract_lanes, preferred_element_type=F32)
            es = []
            for p in range(pairs):
                fill = jnp.where(sink_slot, sink_ref[h * 2 * pairs + 2 * p + side], NEG_INF)
                s = jnp.where(allowed, s_all[p * blk:(p + 1) * blk], fill)
                e = jnp.exp(s - jnp.max(s, axis=-1, keepdims=True))
                es.append(e.astype(BF16))
            e_sides.append(jnp.concatenate(es, axis=0))
        o_den = _dot(jnp.concatenate(e_sides, axis=1), values)
        out = o_den[:, :V7X_LANES] * (1.0 / o_den[:, V7X_LANES:])
        for p in range(pairs):
            o_ref[:, qbase + p * V7X_LANES: qbase + (p + 1) * V7X_LANES] = (
                out[p * blk:(p + 1) * blk].astype(BF16))


def _attention(sinks, qkv, kv_meta, *, d, batch, n_kv, pad, side_casts=()):
    t = qkv.shape[0]
    blk = ATTN_BLOCK
    nb = t // batch // blk
    kvw = kv_meta.shape[1]
    kv_col = d // kvw
    assert kv_col * kvw == d and qkv.shape[1] == d + kvw
    side_in, side_out, side_shapes, side_vmem = _side_cast_specs(side_casts, batch, nb)
    est = 2 * 2 * blk * d * 2 + 3 * 2 * blk * kvw * 2 + 24 * 512 * 256 * 4 + side_vmem
    kern = functools.partial(_attn_kernel, n_kv=n_kv, pad=pad, side_fns=tuple(s.fn for s in side_casts))
    return pl.pallas_call(
        kern,
        grid=(batch, nb),
        in_specs=[
            pl.BlockSpec(memory_space=pltpu.SMEM),
            pl.BlockSpec((blk, d), lambda b, n: (b * nb + n, 0)),
            pl.BlockSpec((blk, kvw), lambda b, n: (b * nb + n, kv_col)),
            pl.BlockSpec((blk, kvw), lambda b, n: (b * nb + jnp.maximum(n - 1, 0), kv_col)),
            pl.BlockSpec((blk, kvw), lambda b, n: (0, 0)),
        ] + side_in,
        out_specs=[pl.BlockSpec((blk, d), lambda b, n: (b * nb + n, 0))] + side_out,
        out_shape=[jax.ShapeDtypeStruct((t, d), BF16)] + side_shapes,
        compiler_params=_params(est),
        name="swa_attention",
    )(sinks, qkv, qkv, qkv, kv_meta, *(s.w for s in side_casts))


def _out_proj_kernel(a_ref, w_ref, h_ref, o_ref):
    o_ref[...] = h_ref[...] + _dot(a_ref[...], w_ref[...])


def _out_proj(a, w, h, *, tm):
    t, d = h.shape
    k = a.shape[1]
    est = 2 * tm * k * 2 + 2 * k * d * 2 + 2 * 2 * tm * d * 4 + tm * d * 4
    return pl.pallas_call(
        _out_proj_kernel,
        grid=(t // tm, 1),
        in_specs=[
            pl.BlockSpec((tm, k), lambda i, j: (i, 0)),
            pl.BlockSpec((k, d), lambda i, j: (0, 0)),
            pl.BlockSpec((tm, d), lambda i, j: (i, 0)),
        ],
        out_specs=pl.BlockSpec((tm, d), lambda i, j: (i, 0)),
        out_shape=jax.ShapeDtypeStruct((t, d), F32),
        compiler_params=_params(est),
        name="attn_out_proj",
    )(a, w, h)


def _rope_tables(first_pos, n_pos):
    pos = jnp.arange(first_pos, first_pos + n_pos, dtype=F32)
    inv = ROPE_THETA ** (-jnp.arange(0, HEAD_DIM, 2, dtype=F32) / HEAD_DIM)
    inv = jnp.concatenate([inv, inv, inv, inv])
    sign = jnp.concatenate([-jnp.ones_like(inv[:HALF_DIM]), jnp.ones_like(inv[:HALF_DIM])] * 2)
    ang = pos[:, None] * inv[None, :]
    return jnp.cos(ang), jnp.sin(ang) * sign


def _scale_q_block(blk, *, d):
    return jnp.concatenate([blk[:, :d] * HEAD_DIM ** -0.5, blk[:, d:]], axis=1).astype(BF16)


def kernel(x, meta_tokens, norm_mix_0, w_in_conv, conv_w, w_out_conv, norm_mlp_0, w_up_0, w_down_0,
           norm_mix_1, w_qkv, attn_sinks, w_o, norm_mlp_1, w_up_1, w_down_1, norm_final):
    batch, seq, d = x.shape
    n_meta = meta_tokens.shape[0]
    n_kv = (w_qkv.shape[1] - d) // (2 * HEAD_DIM)
    pad = (-(seq + n_meta)) % ATTN_BLOCK
    assert seq % ATTN_BLOCK == 0 and pad + n_meta == ATTN_BLOCK
    assert n_meta % (2 * V7X_SUBLANES) == 0 and d // HEAD_DIM == 8 * n_kv

    tm, cc, nc, fc = TOKEN_TILE, CONV_CHUNK, COL_CHUNK, FF_CHUNK
    assert seq % tm == 0 and d % nc == 0
    tiles_per_seq = seq // tm
    n_q_chunks = d // nc

    row = lambda g: g.reshape(1, d)
    g_mix0, g_mlp0, g_mix1, g_mlp1, g_fin = map(row, (norm_mix_0, norm_mlp_0, norm_mix_1, norm_mlp_1, norm_final))
    kw = n_kv * HEAD_DIM
    assert 2 * kw == nc, "k and v heads together must fill exactly one column chunk"
    cos_m, sin_m = _rope_tables(0, n_meta)
    cos_t, sin_t = _rope_tables(n_meta, seq)
    zeros_tail = jnp.zeros((V7X_SUBLANES, d), F32)
    hm, tail_m, wb, wc, wu, w_out = _conv_mixer(
        meta_tokens, g_mix0, [(w_in_conv, 0), (w_in_conv, d), (w_in_conv, 2 * d)], conv_w, w_out_conv,
        zeros_tail, tm=n_meta, nc=META_CONV_CHUNK, tiles_per_seq=1, cast_weights=True)
    h = x.reshape(batch * seq, d)
    assert seq % CONV_TILE == 0
    qkv_cast = _SideCast(w_qkv, w_qkv.shape[1], functools.partial(_scale_q_block, d=d))
    h, _, wu0, wd0, w_ext, wo, wu1, wd1 = _conv_mixer(
        h, g_mix0, [(wb, 0), (wc, 0), (wu, 0)], conv_w, w_out, tail_m[-V7X_SUBLANES:],
        tm=CONV_TILE, nc=cc, tiles_per_seq=seq // CONV_TILE,
        side_casts=(_plain_cast(w_up_0), _plain_cast(w_down_0), qkv_cast, _plain_cast(w_o),
                    _plain_cast(w_up_1), _plain_cast(w_down_1)))

    (hm,) = _mlp(hm, g_mlp0, wu0, wd0, g_fin, tm=n_meta, fc=fc, final_norm=False)
    (h,) = _mlp(h, g_mlp0, wu0, wd0, g_fin, tm=tm, fc=fc, final_norm=False)

    (qkv_m,) = _qkv(hm, g_mix1, w_ext, cos_m, sin_m, tm=n_meta, nc=nc, tiles_per_seq=1)
    kv_meta = jnp.pad(qkv_m[:, d:], ((pad, 0), (0, 0)))
    (qkv,) = _qkv(h, g_mix1, w_ext, cos_t, sin_t, tm=tm, nc=nc, tiles_per_seq=tiles_per_seq)
    (a,) = _attention(attn_sinks, qkv, kv_meta, d=d, batch=batch, n_kv=n_kv, pad=pad)
    h = _out_proj(a, wo, h, tm=OUT_PROJ_TILE)
    (out,) = _mlp(h, g_mlp1, wu1, wd1, g_fin, tm=tm, fc=fc, final_norm=True)
    return out.reshape(batch, seq, d)
```

```python
import functools
from typing import Callable, NamedTuple

import jax
import jax.numpy as jnp
from jax import lax
from jax.experimental import pallas as pl
from jax.experimental.pallas import tpu as pltpu

HEAD_DIM = 64
HALF_DIM = HEAD_DIM // 2
ATTN_BLOCK = 128
ROPE_THETA = 10000.0
RMS_EPS = 1e-5
NEG_INF = -1e30
CONV_WIDTH = 3

V7X_LANES = 128
V7X_SUBLANES = 8
V7X_BF16_ROWS = 16
V7X_VMEM_LIMIT_BYTES = 60000 * 1024

TOKEN_TILE = 1024
CONV_TILE = 1024
OUT_PROJ_TILE = 512
CONV_CHUNK = 512
META_CONV_CHUNK = 256
COL_CHUNK = 512
FF_CHUNK = 1024
QKV_ROW_SPLITS = 8
CONV_NORM_ROW_SPLITS = 2
CONV_ROW_SPLITS = 1
NORM_ROW_SPLITS = 8

BF16 = jnp.bfloat16
F32 = jnp.float32


def _dot(a, b):
    return jnp.dot(a, b, preferred_element_type=F32)


def _rms_norm(x, g):
    var = jnp.mean(x * x, axis=-1, keepdims=True)
    return (x * lax.rsqrt(var + RMS_EPS)) * g


def _row_splits(tm, splits):
    return splits if tm % (splits * 128) == 0 else 1


def _params(est_bytes):
    del est_bytes
    return pltpu.CompilerParams(
        dimension_semantics=("arbitrary", "arbitrary"), vmem_limit_bytes=V7X_VMEM_LIMIT_BYTES)


class _SideCast(NamedTuple):
    w: jax.Array
    out_cols: int
    fn: Callable


def _plain_cast(w):
    return _SideCast(w, w.shape[1], lambda blk: blk.astype(BF16))


def _side_cast_specs(sides, nt, nj):
    n_steps = nt * nj
    in_specs, out_specs, out_shapes, vmem = [], [], [], 0
    for s in sides:
        rows, cols = s.w.shape
        n_blocks = 1
        while 2 * n_blocks <= min(n_steps, rows // V7X_BF16_ROWS) and rows % (2 * n_blocks) == 0:
            n_blocks *= 2
        rb = rows // n_blocks
        assert rb % V7X_BF16_ROWS == 0
        index = lambda i, j, nb=n_blocks: (((i * nj + j) * nb) // n_steps, 0)
        in_specs.append(pl.BlockSpec((rb, cols), index))
        out_specs.append(pl.BlockSpec((rb, s.out_cols), index))
        out_shapes.append(jax.ShapeDtypeStruct((rows, s.out_cols), BF16))
        vmem += 2 * rb * (cols * 4 + s.out_cols * 2)
    return in_specs, out_specs, out_shapes, vmem


def _run_side_casts(fns, in_refs, out_refs):
    for fn, w_ref, o_ref in zip(fns, in_refs, out_refs):
        o_ref[...] = fn(w_ref[...])


class _TileStream:
    def __init__(self, x_hbm, o_hbm, buf, in_sem, out_sem):
        self.x_hbm, self.o_hbm, self.buf, self.in_sem, self.out_sem = x_hbm, o_hbm, buf, in_sem, out_sem
        self.i, self.j = pl.program_id(0), pl.program_id(1)
        self.nt, self.nj = pl.num_programs(0), pl.num_programs(1)
        self.tm = buf.shape[1]
        self.slot = self.i % 2
        self.tile = buf.at[self.slot]

    def _rows(self, tile):
        return pl.ds(pl.multiple_of(tile * self.tm, self.tm), self.tm)

    def _arrival(self, tile, slot):
        return pltpu.make_async_copy(self.x_hbm.at[self._rows(tile), :], self.buf.at[slot], self.in_sem.at[slot])

    def _departure(self, tile, slot):
        return pltpu.make_async_copy(self.buf.at[slot], self.o_hbm.at[self._rows(tile), :], self.out_sem.at[slot])

    def begin_step(self, n_steps):
        i, j, other = self.i, self.j, 1 - self.slot

        @pl.when(j == 0)
        def _():
            @pl.when(i == 0)
            def _():
                self._arrival(0, 0).start()
            self._arrival(i, self.slot).wait()

        @pl.when(j == min(1, n_steps - 1))
        def _():
            @pl.when(i > 0)
            def _():
                self._departure(i - 1, other).wait()

            @pl.when(i + 1 < self.nt)
            def _():
                self._arrival(i + 1, other).start()

    def end_step(self, n_steps):
        i = self.i

        @pl.when(self.j == n_steps - 1)
        def _():
            self._departure(i, self.slot).start()

            @pl.when(i == self.nt - 1)
            def _():
                self._departure(i, self.slot).wait()


def _conv_mixer_kernel(*refs, tiles_per_seq, n_steps, side_fns, cast_weights):
    n = len(side_fns)
    n_w = 4 if cast_weights else 0
    x_hbm, g_ref, wb_ref, wc_ref, wu_ref, cw_ref, wo_ref, tail_in_ref = refs[:8]
    o_hbm, tail_out_ref = refs[8 + n:10 + n]
    w_bf16_refs = refs[10 + 2 * n:10 + 2 * n + n_w]
    hn_ref, carry_ref, buf, in_sem, out_sem = refs[10 + 2 * n + n_w:]
    stream = _TileStream(x_hbm, o_hbm, buf, in_sem, out_sem)
    stream.begin_step(n_steps)
    o_ref = stream.tile
    i = pl.program_id(0)
    j = pl.program_id(1)
    tm = hn_ref.shape[0]
    cw = cw_ref[...]
    first_tile_of_seq = (i % tiles_per_seq) == 0
    tile_prev = jnp.where(first_tile_of_seq, tail_in_ref[...], carry_ref[j])

    weights = (wb_ref, wc_ref, wu_ref, wo_ref)
    if cast_weights:
        weights = tuple(w[...].astype(BF16) for w in weights)
        for w_out_ref, w in zip(w_bf16_refs, weights):
            w_out_ref[...] = w
    wb, wc, wu, wo = weights

    def mix(hn, prev):
        gate_b = _dot(hn, wb[...])
        v = _dot(hn, wc[...]) * _dot(hn, wu[...])
        vcat = jnp.concatenate([prev, v], axis=0)
        v_m1 = pltpu.roll(vcat, 1, axis=0)[V7X_SUBLANES:]
        v_m2 = pltpu.roll(vcat, 2, axis=0)[V7X_SUBLANES:]
        conv = cw[0:1] * v_m2 + cw[1:2] * v_m1 + cw[2:3] * v
        gated = (gate_b * conv).astype(BF16)
        return _dot(gated, wo[...]), v[v.shape[0] - V7X_SUBLANES:]

    def finish(tail):
        carry_ref[j] = tail
        tail_out_ref[...] = tail
        _run_side_casts(side_fns, refs[8:8 + n], refs[10 + n:10 + 2 * n])

    @pl.when(j == 0)
    def _():
        rm = tm // _row_splits(tm, CONV_NORM_ROW_SPLITS)
        prev = tile_prev
        for r0 in range(0, tm, rm):
            rows = slice(r0, r0 + rm)
            x = o_ref[rows, :]
            hn = _rms_norm(x, g_ref[...]).astype(BF16)
            hn_ref[rows, :] = hn
            y, prev = mix(hn, prev)
            o_ref[rows, :] = x + y
        finish(prev)

    @pl.when(j > 0)
    def _():
        rm = tm // _row_splits(tm, CONV_ROW_SPLITS)
        prev = tile_prev
        for r0 in range(0, tm, rm):
            rows = slice(r0, r0 + rm)
            y, prev = mix(hn_ref[rows, :], prev)
            o_ref[rows, :] += y
        finish(prev)

    stream.end_step(n_steps)


def _conv_mixer(x, gain, w_in_parts, conv_w, w_out, tail_in, *, tm, nc, tiles_per_seq, side_casts=(),
                cast_weights=False):
    t, d = x.shape
    nt, nj = t // tm, d // nc
    side_in, side_out, side_shapes, side_vmem = _side_cast_specs(side_casts, nt, nj)
    est = (2 * tm * d * 4 + tm * d * 2 + 2 * 4 * d * nc * 2 + 8 * tm * nc * 4) + side_vmem
    kern = functools.partial(_conv_mixer_kernel, tiles_per_seq=tiles_per_seq, n_steps=nj,
                             side_fns=tuple(s.fn for s in side_casts), cast_weights=cast_weights)
    in_proj_specs = [pl.BlockSpec((d, nc), lambda i, j, c0=col0 // nc: (0, c0 + j)) for _, col0 in w_in_parts]
    w_copy_specs, w_copy_shapes = [], []
    if cast_weights:
        w_copy_specs = [pl.BlockSpec((d, nc), lambda i, j: (0, j))] * 3 + [pl.BlockSpec((nc, d), lambda i, j: (j, 0))]
        w_copy_shapes = [jax.ShapeDtypeStruct((d, d), BF16)] * 4
    return pl.pallas_call(
        kern,
        grid=(nt, nj),
        in_specs=[
            pl.BlockSpec(memory_space=pl.ANY),
            pl.BlockSpec((1, d), lambda i, j: (0, 0)),
            *in_proj_specs,
            pl.BlockSpec((CONV_WIDTH, nc), lambda i, j: (0, j)),
            pl.BlockSpec((nc, d), lambda i, j: (j, 0)),
            pl.BlockSpec((V7X_SUBLANES, nc), lambda i, j: (0, j)),
        ] + side_in,
        out_specs=[
            pl.BlockSpec(memory_space=pl.ANY),
            pl.BlockSpec((V7X_SUBLANES, nc), lambda i, j: (i, j)),
        ] + side_out + w_copy_specs,
        out_shape=[
            jax.ShapeDtypeStruct((t, d), F32),
            jax.ShapeDtypeStruct((nt * V7X_SUBLANES, d), F32),
        ] + side_shapes + w_copy_shapes,
        scratch_shapes=[
            pltpu.VMEM((tm, d), BF16),
            pltpu.VMEM((nj, V7X_SUBLANES, nc), F32),
            pltpu.VMEM((2, tm, d), F32),
            pltpu.SemaphoreType.DMA((2,)),
            pltpu.SemaphoreType.DMA((2,)),
        ],
        compiler_params=_params(est),
        name="conv_mixer",
    )(x, gain, *(w for w, _ in w_in_parts), conv_w, w_out, tail_in, *(s.w for s in side_casts))


def _mlp_kernel(*refs, final_norm, side_fns):
    n = len(side_fns)
    x_ref, g_ref, wu_ref, wd_ref, gf_ref = refs[:5]
    o_ref = refs[5 + n]
    hn_ref = refs[6 + 2 * n]
    j = pl.program_id(1)
    tm = hn_ref.shape[0]
    rm = tm // _row_splits(tm, NORM_ROW_SPLITS)

    def ffn(hn):
        a = jnp.maximum(_dot(hn, wu_ref[...]), 0.0)
        return _dot((a * a).astype(BF16), wd_ref[...])

    side_casts = functools.partial(_run_side_casts, side_fns, refs[5:5 + n], refs[6 + n:6 + 2 * n])

    @pl.when(j == 0)
    def _():
        for r0 in range(0, tm, rm):
            rows = slice(r0, r0 + rm)
            x = x_ref[rows, :]
            hn = _rms_norm(x, g_ref[...]).astype(BF16)
            hn_ref[rows, :] = hn
            o_ref[rows, :] = x + ffn(hn)
        side_casts()

    @pl.when(j > 0)
    def _():
        o_ref[...] += ffn(hn_ref[...])
        side_casts()

    if final_norm:
        @pl.when(j == pl.num_programs(1) - 1)
        def _():
            o_ref[...] = _rms_norm(o_ref[...], gf_ref[...])


def _mlp(x, gain, w_up, w_down, gain_final, *, tm, fc, final_norm, side_casts=()):
    t, d = x.shape
    f = w_up.shape[1]
    nt, nj = t // tm, f // fc
    side_in, side_out, side_shapes, side_vmem = _side_cast_specs(side_casts, nt, nj)
    est = 2 * 2 * tm * d * 4 + tm * d * 2 + 2 * 2 * d * fc * 2 + 3 * tm * fc * 4 + side_vmem
    kern = functools.partial(_mlp_kernel, final_norm=final_norm, side_fns=tuple(s.fn for s in side_casts))
    return pl.pallas_call(
        kern,
        grid=(nt, nj),
        in_specs=[
            pl.BlockSpec((tm, d), lambda i, j: (i, 0)),
            pl.BlockSpec((1, d), lambda i, j: (0, 0)),
            pl.BlockSpec((d, fc), lambda i, j: (0, j)),
            pl.BlockSpec((fc, d), lambda i, j: (j, 0)),
            pl.BlockSpec((1, d), lambda i, j: (0, 0)),
        ] + side_in,
        out_specs=[pl.BlockSpec((tm, d), lambda i, j: (i, 0))] + side_out,
        out_shape=[jax.ShapeDtypeStruct((t, d), F32)] + side_shapes,
        scratch_shapes=[pltpu.VMEM((tm, d), BF16)],
        compiler_params=_params(est),
        name="mlp_final" if final_norm else "mlp",
    )(x, gain, w_up, w_down, gain_final, *(s.w for s in side_casts))


def _qkv_kernel(x_ref, g_ref, w_ref, cos_ref, sin_ref, q_ref, kv_ref, hn_ref):
    j = pl.program_id(1)
    last = pl.num_programs(1) - 1
    lane = lax.broadcasted_iota(jnp.int32, (1, V7X_LANES), 1)
    first_half = (lane % HEAD_DIM) < HALF_DIM
    tm = hn_ref.shape[0]
    rm = tm // _row_splits(tm, QKV_ROW_SPLITS)

    def rope(t, rows):
        rot = jnp.where(first_half,
                        pltpu.roll(t, V7X_LANES - HALF_DIM, axis=1),
                        pltpu.roll(t, HALF_DIM, axis=1))
        return t * cos_ref[rows, :] + rot * sin_ref[rows, :]

    def project_q(rows, hn):
        y = _dot(hn, w_ref[...])
        for c in range(y.shape[1] // V7X_LANES):
            sl = slice(c * V7X_LANES, (c + 1) * V7X_LANES)
            q_ref[rows, sl] = rope(y[:, sl], rows).astype(BF16)

    def project_kv(rows, hn):
        y = _dot(hn, w_ref[...])
        n_slabs = y.shape[1] // V7X_LANES
        half = n_slabs // 2
        for c in range(n_slabs):
            t = y[:, c * V7X_LANES:(c + 1) * V7X_LANES]
            if c < half:
                t = rope(t, rows)
            first = (c + (c // half) * half) * V7X_LANES
            kv_ref[rows, first:first + V7X_LANES] = t.astype(BF16)
            kv_ref[rows, first + half * V7X_LANES:first + (half + 1) * V7X_LANES] = (
                pltpu.roll(t, HEAD_DIM, axis=1).astype(BF16))

    def all_rows(project, hn_of_rows):
        for r0 in range(0, tm, rm):
            rows = slice(r0, r0 + rm)
            project(rows, hn_of_rows(rows))

    def normalised(rows):
        hn = _rms_norm(x_ref[rows, :], g_ref[...]).astype(BF16)
        hn_ref[rows, :] = hn
        return hn

    @pl.when(j == 0)
    def _():
        all_rows(project_q, normalised)

    @pl.when((j > 0) & (j < last))
    def _():
        all_rows(project_q, lambda rows: hn_ref[rows, :])

    @pl.when(j == last)
    def _():
        all_rows(project_kv, lambda rows: hn_ref[rows, :])


def _qkv(x, gain, w_ext, cos_t, sin_t, *, tm, nc, tiles_per_seq):
    t, d = x.shape
    nw = w_ext.shape[1]
    nt, nj = t // tm, nw // nc
    assert nj >= 2 and nw == d + nc
    est = (2 * tm * d * 4 + tm * d * 2 + 2 * d * nc * 2 + 2 * 2 * tm * 128 * 4 + 2 * 3 * tm * nc * 2
           + 4 * tm * nc * 4)
    return pl.pallas_call(
        _qkv_kernel,
        grid=(nt, nj),
        in_specs=[
            pl.BlockSpec((tm, d), lambda i, j: (i, 0)),
            pl.BlockSpec((1, d), lambda i, j: (0, 0)),
            pl.BlockSpec((d, nc), lambda i, j: (0, j)),
            pl.BlockSpec((tm, V7X_LANES), lambda i, j: (i % tiles_per_seq, 0)),
            pl.BlockSpec((tm, V7X_LANES), lambda i, j: (i % tiles_per_seq, 0)),
        ],
        out_specs=[
            pl.BlockSpec((tm, nc), lambda i, j: (i, jnp.minimum(j, nj - 2))),
            pl.BlockSpec((tm, 2 * nc), lambda i, j: (i, 0)),
        ],
        out_shape=[jax.ShapeDtypeStruct((t, d), BF16), jax.ShapeDtypeStruct((t, 2 * nc), BF16)],
        scratch_shapes=[pltpu.VMEM((tm, d), BF16)],
        compiler_params=_params(est),
        name="qkv_rope",
    )(x, gain, w_ext, cos_t, sin_t)


def _attn_kernel(sink_ref, q_ref, kvc_ref, kvp_ref, kvm_ref, o_ref, *, n_kv, pad):
    n = pl.program_id(1)
    blk = ATTN_BLOCK
    kv_prev = jnp.where(n == 0, kvm_ref[...], kvp_ref[...])
    band = jnp.concatenate([kv_prev, kvc_ref[...]], axis=0)

    qi = lax.broadcasted_iota(jnp.int32, (blk, 2 * blk), 0)
    kj = lax.broadcasted_iota(jnp.int32, (blk, 2 * blk), 1)
    allowed = (kj > qi) & (kj <= qi + blk) & (kj + n * blk >= pad)

    sink_slot = lax.broadcasted_iota(jnp.int32, (1, 2 * blk), 1) == 0
    low = lax.broadcasted_iota(jnp.int32, (2 * blk, V7X_LANES), 1) < HEAD_DIM
    not_sink_row = lax.broadcasted_iota(jnp.int32, (V7X_BF16_ROWS, V7X_LANES), 0) > 0
    zero = jnp.zeros((), BF16)
    kv_width = n_kv * HEAD_DIM

    def head_sides(first_col, h, clear_sink_row):
        def slab(col):
            x = band[:, col:col + V7X_LANES]
            if clear_sink_row:
                first = jnp.where(not_sink_row, x[:V7X_BF16_ROWS].astype(F32), 0.0).astype(BF16)
                x = jnp.concatenate([first, x[V7X_BF16_ROWS:]], axis=0)
            return x
        plain = slab(first_col + (h // 2) * V7X_LANES)
        swapped = slab(first_col + kv_width + (h // 2) * V7X_LANES)
        return (plain, swapped) if h % 2 == 0 else (swapped, plain)

    low_row = lax.broadcasted_iota(jnp.int32, (1, V7X_LANES), 1) < HEAD_DIM
    ones_low = jnp.broadcast_to(jnp.where(low_row, 1.0, 0.0).astype(BF16), (2 * blk, V7X_LANES))
    ones_high = jnp.broadcast_to(jnp.where(low_row, 0.0, 1.0).astype(BF16), (2 * blk, V7X_LANES))
    pairs = 4
    contract_lanes = (((1,), (1,)), ((), ()))

    for h in range(n_kv):
        k_low, k_high = head_sides(0, h, clear_sink_row=False)
        v_low, v_high = head_sides(2 * kv_width, h, clear_sink_row=True)
        k_sides = (jnp.where(low, k_low, zero), jnp.where(low, zero, k_high))
        values = jnp.concatenate([
            jnp.concatenate([jnp.where(low, v_low, zero), ones_low], axis=1),
            jnp.concatenate([jnp.where(low, zero, v_high), ones_high], axis=1)], axis=0)
        qbase = h * pairs * V7X_LANES
        qst = jnp.concatenate(
            [q_ref[:, qbase + p * V7X_LANES: qbase + (p + 1) * V7X_LANES] for p in range(pairs)],
            axis=0)
        e_sides = []
        for side in range(2):
            s_all = lax.dot_general(qst, k_sides[side], contract_lanes, preferred_element_type=F32)
            es = []
            for p in range(pairs):
                fill = jnp.where(sink_slot, sink_ref[h * 2 * pairs + 2 * p + side], NEG_INF)
                s = jnp.where(allowed, s_all[p * blk:(p + 1) * blk], fill)
                e = jnp.exp(s - jnp.max(s, axis=-1, keepdims=True))
                es.append(e.astype(BF16))
            e_sides.append(jnp.concatenate(es, axis=0))
        o_den = _dot(jnp.concatenate(e_sides, axis=1), values)
        out = o_den[:, :V7X_LANES] * (1.0 / o_den[:, V7X_LANES:])
        for p in range(pairs):
            o_ref[:, qbase + p * V7X_LANES: qbase + (p + 1) * V7X_LANES] = (
                out[p * blk:(p + 1) * blk].astype(BF16))


def _attention(sinks, q, kv, kv_meta, *, batch, n_kv, pad):
    t, d = q.shape
    blk = ATTN_BLOCK
    nb = t // batch // blk
    kvw = kv.shape[1]
    assert kvw == 4 * n_kv * HEAD_DIM and kv_meta.shape == (blk, kvw)
    est = 2 * 2 * blk * d * 2 + 3 * 2 * blk * kvw * 2 + 24 * 512 * 256 * 4
    return pl.pallas_call(
        functools.partial(_attn_kernel, n_kv=n_kv, pad=pad),
        grid=(batch, nb),
        in_specs=[
            pl.BlockSpec(memory_space=pltpu.SMEM),
            pl.BlockSpec((blk, d), lambda b, n: (b * nb + n, 0)),
            pl.BlockSpec((blk, kvw), lambda b, n: (b * nb + n, 0)),
            pl.BlockSpec((blk, kvw), lambda b, n: (b * nb + jnp.maximum(n - 1, 0), 0)),
            pl.BlockSpec((blk, kvw), lambda b, n: (0, 0)),
        ],
        out_specs=pl.BlockSpec((blk, d), lambda b, n: (b * nb + n, 0)),
        out_shape=jax.ShapeDtypeStruct((t, d), BF16),
        compiler_params=_params(est),
        name="swa_attention",
    )(sinks, q, kv, kv, kv_meta)


def _out_proj_kernel(a_ref, w_ref, h_ref, o_ref):
    o_ref[...] = h_ref[...] + _dot(a_ref[...], w_ref[...])


def _out_proj(a, w, h, *, tm):
    t, d = h.shape
    k = a.shape[1]
    est = 2 * tm * k * 2 + 2 * k * d * 2 + 2 * 2 * tm * d * 4 + tm * d * 4
    return pl.pallas_call(
        _out_proj_kernel,
        grid=(t // tm, 1),
        in_specs=[
            pl.BlockSpec((tm, k), lambda i, j: (i, 0)),
            pl.BlockSpec((k, d), lambda i, j: (0, 0)),
            pl.BlockSpec((tm, d), lambda i, j: (i, 0)),
        ],
        out_specs=pl.BlockSpec((tm, d), lambda i, j: (i, 0)),
        out_shape=jax.ShapeDtypeStruct((t, d), F32),
        compiler_params=_params(est),
        name="attn_out_proj",
    )(a, w, h)


def _rope_tables(first_pos, n_pos):
    pos = jnp.arange(first_pos, first_pos + n_pos, dtype=F32)
    inv = ROPE_THETA ** (-jnp.arange(0, HEAD_DIM, 2, dtype=F32) / HEAD_DIM)
    inv = jnp.concatenate([inv, inv, inv, inv])
    sign = jnp.concatenate([-jnp.ones_like(inv[:HALF_DIM]), jnp.ones_like(inv[:HALF_DIM])] * 2)
    ang = pos[:, None] * inv[None, :]
    return jnp.cos(ang), jnp.sin(ang) * sign


def _scale_q_block(blk, *, d):
    return jnp.concatenate([blk[:, :d] * HEAD_DIM ** -0.5, blk[:, d:]], axis=1).astype(BF16)


def kernel(x, meta_tokens, norm_mix_0, w_in_conv, conv_w, w_out_conv, norm_mlp_0, w_up_0, w_down_0,
           norm_mix_1, w_qkv, attn_sinks, w_o, norm_mlp_1, w_up_1, w_down_1, norm_final):
    batch, seq, d = x.shape
    n_meta = meta_tokens.shape[0]
    n_kv = (w_qkv.shape[1] - d) // (2 * HEAD_DIM)
    pad = (-(seq + n_meta)) % ATTN_BLOCK
    assert seq % ATTN_BLOCK == 0 and pad + n_meta == ATTN_BLOCK
    assert n_meta % (2 * V7X_SUBLANES) == 0 and d // HEAD_DIM == 8 * n_kv

    tm, cc, nc, fc = TOKEN_TILE, CONV_CHUNK, COL_CHUNK, FF_CHUNK
    assert seq % tm == 0 and d % nc == 0
    tiles_per_seq = seq // tm
    n_q_chunks = d // nc

    row = lambda g: g.reshape(1, d)
    g_mix0, g_mlp0, g_mix1, g_mlp1, g_fin = map(row, (norm_mix_0, norm_mlp_0, norm_mix_1, norm_mlp_1, norm_final))
    kw = n_kv * HEAD_DIM
    assert 2 * kw == nc, "k and v heads together must fill exactly one column chunk"
    cos_m, sin_m = _rope_tables(0, n_meta)
    cos_t, sin_t = _rope_tables(n_meta, seq)
    zeros_tail = jnp.zeros((V7X_SUBLANES, d), F32)
    hm, tail_m, wb, wc, wu, w_out = _conv_mixer(
        meta_tokens, g_mix0, [(w_in_conv, 0), (w_in_conv, d), (w_in_conv, 2 * d)], conv_w, w_out_conv,
        zeros_tail, tm=n_meta, nc=META_CONV_CHUNK, tiles_per_seq=1, cast_weights=True)
    h = x.reshape(batch * seq, d)
    assert seq % CONV_TILE == 0
    qkv_cast = _SideCast(w_qkv, w_qkv.shape[1], functools.partial(_scale_q_block, d=d))
    h, _, wu0, wd0, w_ext, wo, wu1, wd1 = _conv_mixer(
        h, g_mix0, [(wb, 0), (wc, 0), (wu, 0)], conv_w, w_out, tail_m[-V7X_SUBLANES:],
        tm=CONV_TILE, nc=cc, tiles_per_seq=seq // CONV_TILE,
        side_casts=(_plain_cast(w_up_0), _plain_cast(w_down_0), qkv_cast, _plain_cast(w_o),
                    _plain_cast(w_up_1), _plain_cast(w_down_1)))

    (hm,) = _mlp(hm, g_mlp0, wu0, wd0, g_fin, tm=n_meta, fc=fc, final_norm=False)
    (h,) = _mlp(h, g_mlp0, wu0, wd0, g_fin, tm=tm, fc=fc, final_norm=False)

    _, kv_m = _qkv(hm, g_mix1, w_ext, cos_m, sin_m, tm=n_meta, nc=nc, tiles_per_seq=1)
    kv_meta = jnp.pad(kv_m, ((pad, 0), (0, 0)))
    q, kv = _qkv(h, g_mix1, w_ext, cos_t, sin_t, tm=tm, nc=nc, tiles_per_seq=tiles_per_seq)
    a = _attention(attn_sinks, q, kv, kv_meta, batch=batch, n_kv=n_kv, pad=pad)
    h = _out_proj(a, wo, h, tm=OUT_PROJ_TILE)
    (out,) = _mlp(h, g_mlp1, wu1, wd1, g_fin, tm=tm, fc=fc, final_norm=True)
    return out.reshape(batch, seq, d)
```

```python
import functools
from typing import Callable, NamedTuple

import jax
import jax.numpy as jnp
from jax import lax
from jax.experimental import pallas as pl
from jax.experimental.pallas import tpu as pltpu

HEAD_DIM = 64
HALF_DIM = HEAD_DIM // 2
ATTN_BLOCK = 128
ROPE_THETA = 10000.0
RMS_EPS = 1e-5
NEG_INF = -1e30
CONV_WIDTH = 3

V7X_LANES = 128
V7X_SUBLANES = 8
V7X_BF16_ROWS = 16
V7X_VMEM_LIMIT_BYTES = 60000 * 1024

TOKEN_TILE = 1024
OUT_PROJ_TILE = 512
CONV_CHUNK = 512
META_CONV_CHUNK = 256
COL_CHUNK = 512
FF_CHUNK = 1024
QKV_ROW_SPLITS = 8
MLP_NORM_ROW_SPLITS = 4
CONV_NORM_ROW_SPLITS = 2

BF16 = jnp.bfloat16
F32 = jnp.float32

_COMPILER_PARAMS = pltpu.CompilerParams(
    dimension_semantics=("arbitrary", "arbitrary"), vmem_limit_bytes=V7X_VMEM_LIMIT_BYTES)


def _dot(a, b):
    return jnp.dot(a, b, preferred_element_type=F32)


def _rms_norm(x, g):
    var = jnp.mean(x * x, axis=-1, keepdims=True)
    return (x * lax.rsqrt(var + RMS_EPS)) * g


def _row_splits(tm, splits):
    return splits if tm % (splits * 128) == 0 else 1


class _SideCast(NamedTuple):
    w: jax.Array
    fn: Callable


def _plain_cast(w):
    return _SideCast(w, lambda blk: blk.astype(BF16))


def _side_cast_specs(sides, nt, nj):
    n_steps = nt * nj
    in_specs, out_specs, out_shapes = [], [], []
    for s in sides:
        rows, cols = s.w.shape
        n_blocks = 1
        while 2 * n_blocks <= min(n_steps, rows // V7X_BF16_ROWS) and rows % (2 * n_blocks) == 0:
            n_blocks *= 2
        rb = rows // n_blocks
        assert rb % V7X_BF16_ROWS == 0
        index = lambda i, j, nb=n_blocks: (((i * nj + j) * nb) // n_steps, 0)
        in_specs.append(pl.BlockSpec((rb, cols), index))
        out_specs.append(pl.BlockSpec((rb, cols), index))
        out_shapes.append(jax.ShapeDtypeStruct((rows, cols), BF16))
    return in_specs, out_specs, out_shapes


def _run_side_casts(fns, in_refs, out_refs):
    for fn, w_ref, o_ref in zip(fns, in_refs, out_refs):
        o_ref[...] = fn(w_ref[...])


class _TileStream:
    def __init__(self, x_hbm, o_hbm, buf, in_sem, out_sem):
        self.x_hbm, self.o_hbm, self.buf, self.in_sem, self.out_sem = x_hbm, o_hbm, buf, in_sem, out_sem
        self.i, self.j = pl.program_id(0), pl.program_id(1)
        self.nt = pl.num_programs(0)
        self.tm = buf.shape[1]
        self.slot = self.i % 2
        self.tile = buf.at[self.slot]

    def _rows(self, tile):
        return pl.ds(pl.multiple_of(tile * self.tm, self.tm), self.tm)

    def _arrival(self, tile, slot):
        return pltpu.make_async_copy(self.x_hbm.at[self._rows(tile), :], self.buf.at[slot], self.in_sem.at[slot])

    def _departure(self, tile, slot):
        return pltpu.make_async_copy(self.buf.at[slot], self.o_hbm.at[self._rows(tile), :], self.out_sem.at[slot])

    def begin_step(self, n_steps):
        i, j, other = self.i, self.j, 1 - self.slot

        @pl.when(j == 0)
        def _():
            @pl.when(i == 0)
            def _():
                self._arrival(0, 0).start()
            self._arrival(i, self.slot).wait()

        @pl.when(j == min(1, n_steps - 1))
        def _():
            @pl.when(i > 0)
            def _():
                self._departure(i - 1, other).wait()

            @pl.when(i + 1 < self.nt)
            def _():
                self._arrival(i + 1, other).start()

    def end_step(self, n_steps):
        i = self.i

        @pl.when(self.j == n_steps - 1)
        def _():
            self._departure(i, self.slot).start()

            @pl.when(i == self.nt - 1)
            def _():
                self._departure(i, self.slot).wait()


def _conv_mixer_kernel(*refs, tiles_per_seq, n_steps, side_fns, cast_weights):
    n = len(side_fns)
    n_w = 4 if cast_weights else 0
    x_hbm, g_ref, wb_ref, wc_ref, wu_ref, cw_ref, wo_ref, tail_in_ref = refs[:8]
    o_hbm, tail_out_ref = refs[8 + n:10 + n]
    w_bf16_refs = refs[10 + 2 * n:10 + 2 * n + n_w]
    hn_ref, carry_ref, buf, in_sem, out_sem = refs[10 + 2 * n + n_w:]
    stream = _TileStream(x_hbm, o_hbm, buf, in_sem, out_sem)
    stream.begin_step(n_steps)
    o_ref = stream.tile
    i = pl.program_id(0)
    j = pl.program_id(1)
    tm = hn_ref.shape[0]
    cw = cw_ref[...]
    first_tile_of_seq = (i % tiles_per_seq) == 0
    tile_prev = jnp.where(first_tile_of_seq, tail_in_ref[...], carry_ref[j])

    weights = (wb_ref, wc_ref, wu_ref, wo_ref)
    if cast_weights:
        weights = tuple(w[...].astype(BF16) for w in weights)
        for w_out_ref, w in zip(w_bf16_refs, weights):
            w_out_ref[...] = w
    wb, wc, wu, wo = weights

    def mix(hn, prev):
        gate_b = _dot(hn, wb[...])
        v = _dot(hn, wc[...]) * _dot(hn, wu[...])
        vcat = jnp.concatenate([prev, v], axis=0)
        v_m1 = pltpu.roll(vcat, 1, axis=0)[V7X_SUBLANES:]
        v_m2 = pltpu.roll(vcat, 2, axis=0)[V7X_SUBLANES:]
        conv = cw[0:1] * v_m2 + cw[1:2] * v_m1 + cw[2:3] * v
        gated = (gate_b * conv).astype(BF16)
        return _dot(gated, wo[...]), v[v.shape[0] - V7X_SUBLANES:]

    def finish(tail):
        carry_ref[j] = tail
        tail_out_ref[...] = tail
        _run_side_casts(side_fns, refs[8:8 + n], refs[10 + n:10 + 2 * n])

    @pl.when(j == 0)
    def _():
        rm = tm // _row_splits(tm, CONV_NORM_ROW_SPLITS)
        prev = tile_prev
        for r0 in range(0, tm, rm):
            rows = slice(r0, r0 + rm)
            x = o_ref[rows, :]
            hn = _rms_norm(x, g_ref[...]).astype(BF16)
            hn_ref[rows, :] = hn
            y, prev = mix(hn, prev)
            o_ref[rows, :] = x + y
        finish(prev)

    @pl.when(j > 0)
    def _():
        y, tail = mix(hn_ref[...], tile_prev)
        o_ref[...] += y
        finish(tail)

    stream.end_step(n_steps)


def _conv_mixer(x, gain, w_in_parts, conv_w, w_out, tail_in, *, tm, nc, tiles_per_seq, side_casts=(),
                cast_weights=False):
    t, d = x.shape
    nt, nj = t // tm, d // nc
    side_in, side_out, side_shapes = _side_cast_specs(side_casts, nt, nj)
    kern = functools.partial(_conv_mixer_kernel, tiles_per_seq=tiles_per_seq, n_steps=nj,
                             side_fns=tuple(s.fn for s in side_casts), cast_weights=cast_weights)
    in_proj_specs = [pl.BlockSpec((d, nc), lambda i, j, c0=col0 // nc: (0, c0 + j)) for _, col0 in w_in_parts]
    w_copy_specs, w_copy_shapes = [], []
    if cast_weights:
        w_copy_specs = [pl.BlockSpec((d, nc), lambda i, j: (0, j))] * 3 + [pl.BlockSpec((nc, d), lambda i, j: (j, 0))]
        w_copy_shapes = [jax.ShapeDtypeStruct((d, d), BF16)] * 4
    return pl.pallas_call(
        kern,
        grid=(nt, nj),
        in_specs=[
            pl.BlockSpec(memory_space=pl.ANY),
            pl.BlockSpec((1, d), lambda i, j: (0, 0)),
            *in_proj_specs,
            pl.BlockSpec((CONV_WIDTH, nc), lambda i, j: (0, j)),
            pl.BlockSpec((nc, d), lambda i, j: (j, 0)),
            pl.BlockSpec((V7X_SUBLANES, nc), lambda i, j: (0, j)),
        ] + side_in,
        out_specs=[
            pl.BlockSpec(memory_space=pl.ANY),
            pl.BlockSpec((V7X_SUBLANES, nc), lambda i, j: (i, j)),
        ] + side_out + w_copy_specs,
        out_shape=[
            jax.ShapeDtypeStruct((t, d), F32),
            jax.ShapeDtypeStruct((nt * V7X_SUBLANES, d), F32),
        ] + side_shapes + w_copy_shapes,
        scratch_shapes=[
            pltpu.VMEM((tm, d), BF16),
            pltpu.VMEM((nj, V7X_SUBLANES, nc), F32),
            pltpu.VMEM((2, tm, d), F32),
            pltpu.SemaphoreType.DMA((2,)),
            pltpu.SemaphoreType.DMA((2,)),
        ],
        compiler_params=_COMPILER_PARAMS,
        name="conv_mixer",
    )(x, gain, *(w for w, _ in w_in_parts), conv_w, w_out, tail_in, *(s.w for s in side_casts))


def _mlp_kernel(x_ref, g_ref, wu_ref, wd_ref, gf_ref, o_ref, hn_ref, *, final_norm):
    j = pl.program_id(1)
    tm = hn_ref.shape[0]
    rm = tm // _row_splits(tm, MLP_NORM_ROW_SPLITS)

    def ffn(hn):
        a = jnp.maximum(_dot(hn, wu_ref[...]), 0.0)
        return _dot((a * a).astype(BF16), wd_ref[...])

    @pl.when(j == 0)
    def _():
        for r0 in range(0, tm, rm):
            rows = slice(r0, r0 + rm)
            x = x_ref[rows, :]
            hn = _rms_norm(x, g_ref[...]).astype(BF16)
            hn_ref[rows, :] = hn
            o_ref[rows, :] = x + ffn(hn)

    @pl.when(j > 0)
    def _():
        o_ref[...] += ffn(hn_ref[...])

    if final_norm:
        @pl.when(j == pl.num_programs(1) - 1)
        def _():
            o_ref[...] = _rms_norm(o_ref[...], gf_ref[...])


def _mlp(x, gain, w_up, w_down, gain_final, *, tm, fc, final_norm):
    t, d = x.shape
    f = w_up.shape[1]
    return pl.pallas_call(
        functools.partial(_mlp_kernel, final_norm=final_norm),
        grid=(t // tm, f // fc),
        in_specs=[
            pl.BlockSpec((tm, d), lambda i, j: (i, 0)),
            pl.BlockSpec((1, d), lambda i, j: (0, 0)),
            pl.BlockSpec((d, fc), lambda i, j: (0, j)),
            pl.BlockSpec((fc, d), lambda i, j: (j, 0)),
            pl.BlockSpec((1, d), lambda i, j: (0, 0)),
        ],
        out_specs=pl.BlockSpec((tm, d), lambda i, j: (i, 0)),
        out_shape=jax.ShapeDtypeStruct((t, d), F32),
        scratch_shapes=[pltpu.VMEM((tm, d), BF16)],
        compiler_params=_COMPILER_PARAMS,
        name="mlp_final" if final_norm else "mlp",
    )(x, gain, w_up, w_down, gain_final)


def _qkv_kernel(x_ref, g_ref, w_ref, cos_ref, sin_ref, q_ref, kv_ref, hn_ref):
    j = pl.program_id(1)
    last = pl.num_programs(1) - 1
    lane = lax.broadcasted_iota(jnp.int32, (1, V7X_LANES), 1)
    first_half = (lane % HEAD_DIM) < HALF_DIM
    tm = hn_ref.shape[0]
    rm = tm // _row_splits(tm, QKV_ROW_SPLITS)

    def rope(t, rows):
        rot = jnp.where(first_half,
                        pltpu.roll(t, V7X_LANES - HALF_DIM, axis=1),
                        pltpu.roll(t, HALF_DIM, axis=1))
        return t * cos_ref[rows, :] + rot * sin_ref[rows, :]

    def project_q(rows, hn):
        y = _dot(hn, w_ref[...])
        for c in range(y.shape[1] // V7X_LANES):
            sl = slice(c * V7X_LANES, (c + 1) * V7X_LANES)
            q_ref[rows, sl] = rope(y[:, sl], rows).astype(BF16)

    def project_kv(rows, hn):
        y = _dot(hn, w_ref[...])
        n_slabs = y.shape[1] // V7X_LANES
        half = n_slabs // 2
        for c in range(n_slabs):
            t = y[:, c * V7X_LANES:(c + 1) * V7X_LANES]
            if c < half:
                t = rope(t, rows)
            first = (c + (c // half) * half) * V7X_LANES
            kv_ref[rows, first:first + V7X_LANES] = t.astype(BF16)
            kv_ref[rows, first + half * V7X_LANES:first + (half + 1) * V7X_LANES] = (
                pltpu.roll(t, HEAD_DIM, axis=1).astype(BF16))

    def all_rows(project, hn_of_rows):
        for r0 in range(0, tm, rm):
            rows = slice(r0, r0 + rm)
            project(rows, hn_of_rows(rows))

    def normalised(rows):
        hn = _rms_norm(x_ref[rows, :], g_ref[...]).astype(BF16)
        hn_ref[rows, :] = hn
        return hn

    @pl.when(j == 0)
    def _():
        all_rows(project_q, normalised)

    @pl.when((j > 0) & (j < last))
    def _():
        all_rows(project_q, lambda rows: hn_ref[rows, :])

    @pl.when(j == last)
    def _():
        all_rows(project_kv, lambda rows: hn_ref[rows, :])


def _qkv(x, gain, w_ext, cos_t, sin_t, *, tm, nc, tiles_per_seq):
    t, d = x.shape
    nw = w_ext.shape[1]
    nt, nj = t // tm, nw // nc
    assert nj >= 2 and nw == d + nc
    return pl.pallas_call(
        _qkv_kernel,
        grid=(nt, nj),
        in_specs=[
            pl.BlockSpec((tm, d), lambda i, j: (i, 0)),
            pl.BlockSpec((1, d), lambda i, j: (0, 0)),
            pl.BlockSpec((d, nc), lambda i, j: (0, j)),
            pl.BlockSpec((tm, V7X_LANES), lambda i, j: (i % tiles_per_seq, 0)),
            pl.BlockSpec((tm, V7X_LANES), lambda i, j: (i % tiles_per_seq, 0)),
        ],
        out_specs=[
            pl.BlockSpec((tm, nc), lambda i, j: (i, jnp.minimum(j, nj - 2))),
            pl.BlockSpec((tm, 2 * nc), lambda i, j: (i, 0)),
        ],
        out_shape=[jax.ShapeDtypeStruct((t, d), BF16), jax.ShapeDtypeStruct((t, 2 * nc), BF16)],
        scratch_shapes=[pltpu.VMEM((tm, d), BF16)],
        compiler_params=_COMPILER_PARAMS,
        name="qkv_rope",
    )(x, gain, w_ext, cos_t, sin_t)


def _attn_kernel(sink_ref, q_ref, kvc_ref, kvp_ref, kvm_ref, o_ref, *, n_kv, pad):
    n = pl.program_id(1)
    blk = ATTN_BLOCK
    kv_prev = jnp.where(n == 0, kvm_ref[...], kvp_ref[...])
    band = jnp.concatenate([kv_prev, kvc_ref[...]], axis=0)

    qi = lax.broadcasted_iota(jnp.int32, (blk, 2 * blk), 0)
    kj = lax.broadcasted_iota(jnp.int32, (blk, 2 * blk), 1)
    allowed = (kj > qi) & (kj <= qi + blk) & (kj + n * blk >= pad)

    sink_slot = lax.broadcasted_iota(jnp.int32, (1, 2 * blk), 1) == 0
    low = lax.broadcasted_iota(jnp.int32, (2 * blk, V7X_LANES), 1) < HEAD_DIM
    not_sink_row = lax.broadcasted_iota(jnp.int32, (V7X_BF16_ROWS, V7X_LANES), 0) > 0
    zero = jnp.zeros((), BF16)
    kv_width = n_kv * HEAD_DIM

    def head_sides(first_col, h, clear_sink_row):
        def slab(col):
            x = band[:, col:col + V7X_LANES]
            if clear_sink_row:
                first = jnp.where(not_sink_row, x[:V7X_BF16_ROWS].astype(F32), 0.0).astype(BF16)
                x = jnp.concatenate([first, x[V7X_BF16_ROWS:]], axis=0)
            return x
        plain = slab(first_col + (h // 2) * V7X_LANES)
        swapped = slab(first_col + kv_width + (h // 2) * V7X_LANES)
        return (plain, swapped) if h % 2 == 0 else (swapped, plain)

    low_row = lax.broadcasted_iota(jnp.int32, (1, V7X_LANES), 1) < HEAD_DIM
    ones_low = jnp.broadcast_to(jnp.where(low_row, 1.0, 0.0).astype(BF16), (2 * blk, V7X_LANES))
    ones_high = jnp.broadcast_to(jnp.where(low_row, 0.0, 1.0).astype(BF16), (2 * blk, V7X_LANES))
    pairs = 4
    contract_lanes = (((1,), (1,)), ((), ()))

    for h in range(n_kv):
        k_low, k_high = head_sides(0, h, clear_sink_row=False)
        v_low, v_high = head_sides(2 * kv_width, h, clear_sink_row=True)
        k_sides = (jnp.where(low, k_low, zero), jnp.where(low, zero, k_high))
        values = jnp.concatenate([
            jnp.concatenate([jnp.where(low, v_low, zero), ones_low], axis=1),
            jnp.concatenate([jnp.where(low, zero, v_high), ones_high], axis=1)], axis=0)
        qbase = h * pairs * V7X_LANES
        qst = jnp.concatenate(
            [q_ref[:, qbase + p * V7X_LANES: qbase + (p + 1) * V7X_LANES] for p in range(pairs)],
            axis=0)
        e_sides = []
        for side in range(2):
            s_all = lax.dot_general(qst, k_sides[side], contract_lanes, preferred_element_type=F32)
            es = []
            for p in range(pairs):
                fill = jnp.where(sink_slot, sink_ref[h * 2 * pairs + 2 * p + side], NEG_INF)
                s = jnp.where(allowed, s_all[p * blk:(p + 1) * blk], fill)
                e = jnp.exp(s - jnp.max(s, axis=-1, keepdims=True))
                es.append(e.astype(BF16))
            e_sides.append(jnp.concatenate(es, axis=0))
        o_den = _dot(jnp.concatenate(e_sides, axis=1), values)
        out = o_den[:, :V7X_LANES] * (1.0 / o_den[:, V7X_LANES:])
        for p in range(pairs):
            o_ref[:, qbase + p * V7X_LANES: qbase + (p + 1) * V7X_LANES] = (
                out[p * blk:(p + 1) * blk].astype(BF16))


def _attention(sinks, q, kv, kv_meta, *, batch, n_kv, pad):
    t, d = q.shape
    blk = ATTN_BLOCK
    nb = t // batch // blk
    kvw = kv.shape[1]
    assert kvw == 4 * n_kv * HEAD_DIM and kv_meta.shape == (blk, kvw)
    return pl.pallas_call(
        functools.partial(_attn_kernel, n_kv=n_kv, pad=pad),
        grid=(batch, nb),
        in_specs=[
            pl.BlockSpec(memory_space=pltpu.SMEM),
            pl.BlockSpec((blk, d), lambda b, n: (b * nb + n, 0)),
            pl.BlockSpec((blk, kvw), lambda b, n: (b * nb + n, 0)),
            pl.BlockSpec((blk, kvw), lambda b, n: (b * nb + jnp.maximum(n - 1, 0), 0)),
            pl.BlockSpec((blk, kvw), lambda b, n: (0, 0)),
        ],
        out_specs=pl.BlockSpec((blk, d), lambda b, n: (b * nb + n, 0)),
        out_shape=jax.ShapeDtypeStruct((t, d), BF16),
        compiler_params=_COMPILER_PARAMS,
        name="swa_attention",
    )(sinks, q, kv, kv, kv_meta)


def _out_proj_kernel(a_ref, w_ref, h_ref, o_ref):
    o_ref[...] = h_ref[...] + _dot(a_ref[...], w_ref[...])


def _out_proj(a, w, h, *, tm):
    t, d = h.shape
    k = a.shape[1]
    return pl.pallas_call(
        _out_proj_kernel,
        grid=(t // tm, 1),
        in_specs=[
            pl.BlockSpec((tm, k), lambda i, j: (i, 0)),
            pl.BlockSpec((k, d), lambda i, j: (0, 0)),
            pl.BlockSpec((tm, d), lambda i, j: (i, 0)),
        ],
        out_specs=pl.BlockSpec((tm, d), lambda i, j: (i, 0)),
        out_shape=jax.ShapeDtypeStruct((t, d), F32),
        compiler_params=_COMPILER_PARAMS,
        name="attn_out_proj",
    )(a, w, h)


def _rope_tables(first_pos, n_pos):
    pos = jnp.arange(first_pos, first_pos + n_pos, dtype=F32)
    inv = ROPE_THETA ** (-jnp.arange(0, HEAD_DIM, 2, dtype=F32) / HEAD_DIM)
    inv = jnp.concatenate([inv, inv, inv, inv])
    sign = jnp.concatenate([-jnp.ones_like(inv[:HALF_DIM]), jnp.ones_like(inv[:HALF_DIM])] * 2)
    ang = pos[:, None] * inv[None, :]
    return jnp.cos(ang), jnp.sin(ang) * sign


def _scale_q_block(blk, *, d):
    return jnp.concatenate([blk[:, :d] * HEAD_DIM ** -0.5, blk[:, d:]], axis=1).astype(BF16)


def kernel(x, meta_tokens, norm_mix_0, w_in_conv, conv_w, w_out_conv, norm_mlp_0, w_up_0, w_down_0,
           norm_mix_1, w_qkv, attn_sinks, w_o, norm_mlp_1, w_up_1, w_down_1, norm_final):
    batch, seq, d = x.shape
    n_meta = meta_tokens.shape[0]
    n_kv = (w_qkv.shape[1] - d) // (2 * HEAD_DIM)
    pad = (-(seq + n_meta)) % ATTN_BLOCK
    assert seq % ATTN_BLOCK == 0 and pad + n_meta == ATTN_BLOCK
    assert n_meta % (2 * V7X_SUBLANES) == 0 and d // HEAD_DIM == 8 * n_kv

    tm, cc, nc, fc = TOKEN_TILE, CONV_CHUNK, COL_CHUNK, FF_CHUNK
    assert seq % tm == 0 and d % nc == 0
    assert 2 * n_kv * HEAD_DIM == nc, "k and v heads together must fill exactly one column chunk"
    tiles_per_seq = seq // tm

    row = lambda g: g.reshape(1, d)
    g_mix0, g_mlp0, g_mix1, g_mlp1, g_fin = map(row, (norm_mix_0, norm_mlp_0, norm_mix_1, norm_mlp_1, norm_final))
    cos_m, sin_m = _rope_tables(0, n_meta)
    cos_t, sin_t = _rope_tables(n_meta, seq)
    zeros_tail = jnp.zeros((V7X_SUBLANES, d), F32)
    hm, tail_m, wb, wc, wu, w_out = _conv_mixer(
        meta_tokens, g_mix0, [(w_in_conv, 0), (w_in_conv, d), (w_in_conv, 2 * d)], conv_w, w_out_conv,
        zeros_tail, tm=n_meta, nc=META_CONV_CHUNK, tiles_per_seq=1, cast_weights=True)
    h = x.reshape(batch * seq, d)
    qkv_cast = _SideCast(w_qkv, functools.partial(_scale_q_block, d=d))
    h, _, wu0, wd0, w_ext, wo, wu1, wd1 = _conv_mixer(
        h, g_mix0, [(wb, 0), (wc, 0), (wu, 0)], conv_w, w_out, tail_m[-V7X_SUBLANES:],
        tm=tm, nc=cc, tiles_per_seq=tiles_per_seq,
        side_casts=(_plain_cast(w_up_0), _plain_cast(w_down_0), qkv_cast, _plain_cast(w_o),
                    _plain_cast(w_up_1), _plain_cast(w_down_1)))

    hm = _mlp(hm, g_mlp0, wu0, wd0, g_fin, tm=n_meta, fc=fc, final_norm=False)
    h = _mlp(h, g_mlp0, wu0, wd0, g_fin, tm=tm, fc=fc, final_norm=False)

    _, kv_m = _qkv(hm, g_mix1, w_ext, cos_m, sin_m, tm=n_meta, nc=nc, tiles_per_seq=1)
    kv_meta = jnp.pad(kv_m, ((pad, 0), (0, 0)))
    q, kv = _qkv(h, g_mix1, w_ext, cos_t, sin_t, tm=tm, nc=nc, tiles_per_seq=tiles_per_seq)
    a = _attention(attn_sinks, q, kv, kv_meta, batch=batch, n_kv=n_kv, pad=pad)
    h = _out_proj(a, wo, h, tm=OUT_PROJ_TILE)
    out = _mlp(h, g_mlp1, wu1, wd1, g_fin, tm=tm, fc=fc, final_norm=True)
    return out.reshape(batch, seq, d)
```

```python
import functools
from typing import Callable, NamedTuple

import jax
import jax.numpy as jnp
from jax import lax
from jax.experimental import pallas as pl
from jax.experimental.pallas import tpu as pltpu

HEAD_DIM = 64
HALF_DIM = HEAD_DIM // 2
ATTN_BLOCK = 128
ROPE_THETA = 10000.0
RMS_EPS = 1e-5
NEG_INF = -1e30
CONV_WIDTH = 3

V7X_LANES = 128
V7X_SUBLANES = 8
V7X_BF16_ROWS = 16
V7X_VMEM_LIMIT_BYTES = 60000 * 1024

TOKEN_TILE = 1024
OUT_PROJ_TILE = 512
CONV_CHUNK = 512
META_CONV_CHUNK = 256
COL_CHUNK = 512
FF_CHUNK = 1024
QKV_ROW_SPLITS = 4
MLP_NORM_ROW_SPLITS = 4
CONV_NORM_ROW_SPLITS = 2

BF16 = jnp.bfloat16
F32 = jnp.float32

_COMPILER_PARAMS = pltpu.CompilerParams(
    dimension_semantics=("arbitrary", "arbitrary"), vmem_limit_bytes=V7X_VMEM_LIMIT_BYTES)


def _dot(a, b):
    return jnp.dot(a, b, preferred_element_type=F32)


def _rms_norm(x, g):
    var = jnp.mean(x * x, axis=-1, keepdims=True)
    return (x * lax.rsqrt(var + RMS_EPS)) * g


def _row_splits(tm, splits):
    return splits if tm % (splits * 128) == 0 else 1


class _SideCast(NamedTuple):
    w: jax.Array
    fn: Callable


def _plain_cast(w):
    return _SideCast(w, lambda blk: blk.astype(BF16))


def _side_cast_specs(sides, nt, nj):
    n_steps = nt * nj
    in_specs, out_specs, out_shapes = [], [], []
    for s in sides:
        rows, cols = s.w.shape
        n_blocks = 1
        while 2 * n_blocks <= min(n_steps, rows // V7X_BF16_ROWS) and rows % (2 * n_blocks) == 0:
            n_blocks *= 2
        rb = rows // n_blocks
        assert rb % V7X_BF16_ROWS == 0
        index = lambda i, j, nb=n_blocks: (((i * nj + j) * nb) // n_steps, 0)
        in_specs.append(pl.BlockSpec((rb, cols), index))
        out_specs.append(pl.BlockSpec((rb, cols), index))
        out_shapes.append(jax.ShapeDtypeStruct((rows, cols), BF16))
    return in_specs, out_specs, out_shapes


def _run_side_casts(fns, in_refs, out_refs):
    for fn, w_ref, o_ref in zip(fns, in_refs, out_refs):
        o_ref[...] = fn(w_ref[...])


class _TileStream:
    def __init__(self, x_hbm, o_hbm, buf, in_sem, out_sem):
        self.x_hbm, self.o_hbm, self.buf, self.in_sem, self.out_sem = x_hbm, o_hbm, buf, in_sem, out_sem
        self.i, self.j = pl.program_id(0), pl.program_id(1)
        self.nt = pl.num_programs(0)
        self.tm = buf.shape[1]
        self.slot = self.i % 2
        self.tile = buf.at[self.slot]

    def _rows(self, tile):
        return pl.ds(pl.multiple_of(tile * self.tm, self.tm), self.tm)

    def _arrival(self, tile, slot):
        return pltpu.make_async_copy(self.x_hbm.at[self._rows(tile), :], self.buf.at[slot], self.in_sem.at[slot])

    def _departure(self, tile, slot):
        return pltpu.make_async_copy(self.buf.at[slot], self.o_hbm.at[self._rows(tile), :], self.out_sem.at[slot])

    def begin_step(self, n_steps):
        i, j, other = self.i, self.j, 1 - self.slot

        @pl.when(j == 0)
        def _():
            @pl.when(i == 0)
            def _():
                self._arrival(0, 0).start()
            self._arrival(i, self.slot).wait()

        @pl.when(j == min(1, n_steps - 1))
        def _():
            @pl.when(i > 0)
            def _():
                self._departure(i - 1, other).wait()

            @pl.when(i + 1 < self.nt)
            def _():
                self._arrival(i + 1, other).start()

    def end_step(self, n_steps):
        i = self.i

        @pl.when(self.j == n_steps - 1)
        def _():
            self._departure(i, self.slot).start()

            @pl.when(i == self.nt - 1)
            def _():
                self._departure(i, self.slot).wait()


def _conv_mixer_kernel(*refs, tiles_per_seq, n_steps, side_fns, cast_weights):
    n = len(side_fns)
    n_w = 4 if cast_weights else 0
    x_hbm, g_ref, wb_ref, wc_ref, wu_ref, cw_ref, wo_ref, tail_in_ref = refs[:8]
    o_hbm, tail_out_ref = refs[8 + n:10 + n]
    w_bf16_refs = refs[10 + 2 * n:10 + 2 * n + n_w]
    hn_ref, carry_ref, buf, in_sem, out_sem = refs[10 + 2 * n + n_w:]
    stream = _TileStream(x_hbm, o_hbm, buf, in_sem, out_sem)
    stream.begin_step(n_steps)
    o_ref = stream.tile
    i = pl.program_id(0)
    j = pl.program_id(1)
    tm = hn_ref.shape[0]
    cw = cw_ref[...]
    first_tile_of_seq = (i % tiles_per_seq) == 0
    tile_prev = jnp.where(first_tile_of_seq, tail_in_ref[...], carry_ref[j])

    weights = (wb_ref, wc_ref, wu_ref, wo_ref)
    if cast_weights:
        weights = tuple(w[...].astype(BF16) for w in weights)
        for w_out_ref, w in zip(w_bf16_refs, weights):
            w_out_ref[...] = w
    wb, wc, wu, wo = weights

    def mix(hn, prev):
        gate_b = _dot(hn, wb[...])
        v = _dot(hn, wc[...]) * _dot(hn, wu[...])
        vcat = jnp.concatenate([prev, v], axis=0)
        v_m1 = pltpu.roll(vcat, 1, axis=0)[V7X_SUBLANES:]
        v_m2 = pltpu.roll(vcat, 2, axis=0)[V7X_SUBLANES:]
        conv = cw[0:1] * v_m2 + cw[1:2] * v_m1 + cw[2:3] * v
        gated = (gate_b * conv).astype(BF16)
        return _dot(gated, wo[...]), v[v.shape[0] - V7X_SUBLANES:]

    def finish(tail):
        carry_ref[j] = tail
        tail_out_ref[...] = tail
        _run_side_casts(side_fns, refs[8:8 + n], refs[10 + n:10 + 2 * n])

    @pl.when(j == 0)
    def _():
        rm = tm // _row_splits(tm, CONV_NORM_ROW_SPLITS)
        prev = tile_prev
        for r0 in range(0, tm, rm):
            rows = slice(r0, r0 + rm)
            x = o_ref[rows, :]
            hn = _rms_norm(x, g_ref[...]).astype(BF16)
            hn_ref[rows, :] = hn
            y, prev = mix(hn, prev)
            o_ref[rows, :] = x + y
        finish(prev)

    @pl.when(j > 0)
    def _():
        y, tail = mix(hn_ref[...], tile_prev)
        o_ref[...] += y
        finish(tail)

    stream.end_step(n_steps)


def _conv_mixer(x, gain, w_in_parts, conv_w, w_out, tail_in, *, tm, nc, tiles_per_seq, side_casts=(),
                cast_weights=False):
    t, d = x.shape
    nt, nj = t // tm, d // nc
    side_in, side_out, side_shapes = _side_cast_specs(side_casts, nt, nj)
    kern = functools.partial(_conv_mixer_kernel, tiles_per_seq=tiles_per_seq, n_steps=nj,
                             side_fns=tuple(s.fn for s in side_casts), cast_weights=cast_weights)
    in_proj_specs = [pl.BlockSpec((d, nc), lambda i, j, c0=col0 // nc: (0, c0 + j)) for _, col0 in w_in_parts]
    w_copy_specs, w_copy_shapes = [], []
    if cast_weights:
        w_copy_specs = [pl.BlockSpec((d, nc), lambda i, j: (0, j))] * 3 + [pl.BlockSpec((nc, d), lambda i, j: (j, 0))]
        w_copy_shapes = [jax.ShapeDtypeStruct((d, d), BF16)] * 4
    return pl.pallas_call(
        kern,
        grid=(nt, nj),
        in_specs=[
            pl.BlockSpec(memory_space=pl.ANY),
            pl.BlockSpec((1, d), lambda i, j: (0, 0)),
            *in_proj_specs,
            pl.BlockSpec((CONV_WIDTH, nc), lambda i, j: (0, j)),
            pl.BlockSpec((nc, d), lambda i, j: (j, 0)),
            pl.BlockSpec((V7X_SUBLANES, nc), lambda i, j: (0, j)),
        ] + side_in,
        out_specs=[
            pl.BlockSpec(memory_space=pl.ANY),
            pl.BlockSpec((V7X_SUBLANES, nc), lambda i, j: (i, j)),
        ] + side_out + w_copy_specs,
        out_shape=[
            jax.ShapeDtypeStruct((t, d), F32),
            jax.ShapeDtypeStruct((nt * V7X_SUBLANES, d), F32),
        ] + side_shapes + w_copy_shapes,
        scratch_shapes=[
            pltpu.VMEM((tm, d), BF16),
            pltpu.VMEM((nj, V7X_SUBLANES, nc), F32),
            pltpu.VMEM((2, tm, d), F32),
            pltpu.SemaphoreType.DMA((2,)),
            pltpu.SemaphoreType.DMA((2,)),
        ],
        compiler_params=_COMPILER_PARAMS,
        name="conv_mixer",
    )(x, gain, *(w for w, _ in w_in_parts), conv_w, w_out, tail_in, *(s.w for s in side_casts))


def _mlp_kernel(x_ref, g_ref, wu_ref, wd_ref, gf_ref, o_ref, hn_ref, *, final_norm):
    j = pl.program_id(1)
    tm = hn_ref.shape[0]
    rm = tm // _row_splits(tm, MLP_NORM_ROW_SPLITS)

    def ffn(hn):
        a = jnp.maximum(_dot(hn, wu_ref[...]), 0.0)
        return _dot((a * a).astype(BF16), wd_ref[...])

    @pl.when(j == 0)
    def _():
        for r0 in range(0, tm, rm):
            rows = slice(r0, r0 + rm)
            x = x_ref[rows, :]
            hn = _rms_norm(x, g_ref[...]).astype(BF16)
            hn_ref[rows, :] = hn
            o_ref[rows, :] = x + ffn(hn)

    @pl.when(j > 0)
    def _():
        o_ref[...] += ffn(hn_ref[...])

    if final_norm:
        @pl.when(j == pl.num_programs(1) - 1)
        def _():
            o_ref[...] = _rms_norm(o_ref[...], gf_ref[...])


def _mlp(x, gain, w_up, w_down, gain_final, *, tm, fc, final_norm):
    t, d = x.shape
    f = w_up.shape[1]
    return pl.pallas_call(
        functools.partial(_mlp_kernel, final_norm=final_norm),
        grid=(t // tm, f // fc),
        in_specs=[
            pl.BlockSpec((tm, d), lambda i, j: (i, 0)),
            pl.BlockSpec((1, d), lambda i, j: (0, 0)),
            pl.BlockSpec((d, fc), lambda i, j: (0, j)),
            pl.BlockSpec((fc, d), lambda i, j: (j, 0)),
            pl.BlockSpec((1, d), lambda i, j: (0, 0)),
        ],
        out_specs=pl.BlockSpec((tm, d), lambda i, j: (i, 0)),
        out_shape=jax.ShapeDtypeStruct((t, d), F32),
        scratch_shapes=[pltpu.VMEM((tm, d), BF16)],
        compiler_params=_COMPILER_PARAMS,
        name="mlp_final" if final_norm else "mlp",
    )(x, gain, w_up, w_down, gain_final)


def _qkv_kernel(x_ref, g_ref, w_ref, cos_ref, sin_ref, q_ref, kv_ref, *, nc):
    lane = lax.broadcasted_iota(jnp.int32, (1, V7X_LANES), 1)
    first_half = (lane % HEAD_DIM) < HALF_DIM
    tm, d = x_ref.shape
    slabs = nc // V7X_LANES
    half = slabs // 2
    rm = tm // _row_splits(tm, QKV_ROW_SPLITS)

    def rope(t, rows):
        rot = jnp.where(first_half,
                        pltpu.roll(t, V7X_LANES - HALF_DIM, axis=1),
                        pltpu.roll(t, HALF_DIM, axis=1))
        return t * cos_ref[rows, :] + rot * sin_ref[rows, :]

    for r0 in range(0, tm, rm):
        rows = slice(r0, r0 + rm)
        hn = _rms_norm(x_ref[rows, :], g_ref[...]).astype(BF16)
        for c0 in range(0, d, nc):
            y = _dot(hn, w_ref[:, c0:c0 + nc])
            for s in range(slabs):
                sl = slice(s * V7X_LANES, (s + 1) * V7X_LANES)
                q_ref[rows, c0 + s * V7X_LANES:c0 + (s + 1) * V7X_LANES] = rope(y[:, sl], rows).astype(BF16)
        y = _dot(hn, w_ref[:, d:d + nc])
        for s in range(slabs):
            t = y[:, s * V7X_LANES:(s + 1) * V7X_LANES]
            if s < half:
                t = rope(t, rows)
            first = (s + (s // half) * half) * V7X_LANES
            kv_ref[rows, first:first + V7X_LANES] = t.astype(BF16)
            kv_ref[rows, first + half * V7X_LANES:first + (half + 1) * V7X_LANES] = (
                pltpu.roll(t, HEAD_DIM, axis=1).astype(BF16))


def _qkv(x, gain, w_ext, cos_t, sin_t, *, tm, nc, tiles_per_seq):
    t, d = x.shape
    nw = w_ext.shape[1]
    assert nw == d + nc and d % nc == 0
    return pl.pallas_call(
        functools.partial(_qkv_kernel, nc=nc),
        grid=(t // tm, 1),
        in_specs=[
            pl.BlockSpec((tm, d), lambda i, j: (i, 0)),
            pl.BlockSpec((1, d), lambda i, j: (0, 0)),
            pl.BlockSpec((d, nw), lambda i, j: (0, 0)),
            pl.BlockSpec((tm, V7X_LANES), lambda i, j: (i % tiles_per_seq, 0)),
            pl.BlockSpec((tm, V7X_LANES), lambda i, j: (i % tiles_per_seq, 0)),
        ],
        out_specs=[
            pl.BlockSpec((tm, d), lambda i, j: (i, 0)),
            pl.BlockSpec((tm, 2 * nc), lambda i, j: (i, 0)),
        ],
        out_shape=[jax.ShapeDtypeStruct((t, d), BF16), jax.ShapeDtypeStruct((t, 2 * nc), BF16)],
        compiler_params=_COMPILER_PARAMS,
        name="qkv_rope",
    )(x, gain, w_ext, cos_t, sin_t)


def _attn_kernel(sink_ref, q_ref, kvc_ref, kvp_ref, kvm_ref, o_ref, *, n_kv, pad):
    n = pl.program_id(1)
    blk = ATTN_BLOCK
    kv_prev = jnp.where(n == 0, kvm_ref[...], kvp_ref[...])
    band = jnp.concatenate([kv_prev, kvc_ref[...]], axis=0)

    qi = lax.broadcasted_iota(jnp.int32, (blk, 2 * blk), 0)
    kj = lax.broadcasted_iota(jnp.int32, (blk, 2 * blk), 1)
    allowed = (kj > qi) & (kj <= qi + blk) & (kj + n * blk >= pad)

    sink_slot = lax.broadcasted_iota(jnp.int32, (1, 2 * blk), 1) == 0
    low = lax.broadcasted_iota(jnp.int32, (2 * blk, V7X_LANES), 1) < HEAD_DIM
    not_sink_row = lax.broadcasted_iota(jnp.int32, (V7X_BF16_ROWS, V7X_LANES), 0) > 0
    zero = jnp.zeros((), BF16)
    kv_width = n_kv * HEAD_DIM

    def head_sides(first_col, h, clear_sink_row):
        def slab(col):
            x = band[:, col:col + V7X_LANES]
            if clear_sink_row:
                first = jnp.where(not_sink_row, x[:V7X_BF16_ROWS].astype(F32), 0.0).astype(BF16)
                x = jnp.concatenate([first, x[V7X_BF16_ROWS:]], axis=0)
            return x
        plain = slab(first_col + (h // 2) * V7X_LANES)
        swapped = slab(first_col + kv_width + (h // 2) * V7X_LANES)
        return (plain, swapped) if h % 2 == 0 else (swapped, plain)

    low_row = lax.broadcasted_iota(jnp.int32, (1, V7X_LANES), 1) < HEAD_DIM
    ones_low = jnp.broadcast_to(jnp.where(low_row, 1.0, 0.0).astype(BF16), (2 * blk, V7X_LANES))
    ones_high = jnp.broadcast_to(jnp.where(low_row, 0.0, 1.0).astype(BF16), (2 * blk, V7X_LANES))
    pairs = 4
    contract_lanes = (((1,), (1,)), ((), ()))

    for h in range(n_kv):
        k_low, k_high = head_sides(0, h, clear_sink_row=False)
        v_low, v_high = head_sides(2 * kv_width, h, clear_sink_row=True)
        k_sides = (jnp.where(low, k_low, zero), jnp.where(low, zero, k_high))
        values = jnp.concatenate([
            jnp.concatenate([jnp.where(low, v_low, zero), ones_low], axis=1),
            jnp.concatenate([jnp.where(low, zero, v_high), ones_high], axis=1)], axis=0)
        qbase = h * pairs * V7X_LANES
        qst = jnp.concatenate(
            [q_ref[:, qbase + p * V7X_LANES: qbase + (p + 1) * V7X_LANES] for p in range(pairs)],
            axis=0)
        e_sides = []
        for side in range(2):
            s_all = lax.dot_general(qst, k_sides[side], contract_lanes, preferred_element_type=F32)
            es = []
            for p in range(pairs):
                fill = jnp.where(sink_slot, sink_ref[h * 2 * pairs + 2 * p + side], NEG_INF)
                s = jnp.where(allowed, s_all[p * blk:(p + 1) * blk], fill)
                e = jnp.exp(s - jnp.max(s, axis=-1, keepdims=True))
                es.append(e.astype(BF16))
            e_sides.append(jnp.concatenate(es, axis=0))
        o_den = _dot(jnp.concatenate(e_sides, axis=1), values)
        out = o_den[:, :V7X_LANES] * (1.0 / o_den[:, V7X_LANES:])
        for p in range(pairs):
            o_ref[:, qbase + p * V7X_LANES: qbase + (p + 1) * V7X_LANES] = (
                out[p * blk:(p + 1) * blk].astype(BF16))


def _attention(sinks, q, kv, kv_meta, *, batch, n_kv, pad):
    t, d = q.shape
    blk = ATTN_BLOCK
    nb = t // batch // blk
    kvw = kv.shape[1]
    assert kvw == 4 * n_kv * HEAD_DIM and kv_meta.shape == (blk, kvw)
    return pl.pallas_call(
        functools.partial(_attn_kernel, n_kv=n_kv, pad=pad),
        grid=(batch, nb),
        in_specs=[
            pl.BlockSpec(memory_space=pltpu.SMEM),
            pl.BlockSpec((blk, d), lambda b, n: (b * nb + n, 0)),
            pl.BlockSpec((blk, kvw), lambda b, n: (b * nb + n, 0)),
            pl.BlockSpec((blk, kvw), lambda b, n: (b * nb + jnp.maximum(n - 1, 0), 0)),
            pl.BlockSpec((blk, kvw), lambda b, n: (0, 0)),
        ],
        out_specs=pl.BlockSpec((blk, d), lambda b, n: (b * nb + n, 0)),
        out_shape=jax.ShapeDtypeStruct((t, d), BF16),
        compiler_params=_COMPILER_PARAMS,
        name="swa_attention",
    )(sinks, q, kv, kv, kv_meta)


def _out_proj_kernel(a_ref, w_ref, h_ref, o_ref):
    o_ref[...] = h_ref[...] + _dot(a_ref[...], w_ref[...])


def _out_proj(a, w, h, *, tm):
    t, d = h.shape
    k = a.shape[1]
    return pl.pallas_call(
        _out_proj_kernel,
        grid=(t // tm, 1),
        in_specs=[
            pl.BlockSpec((tm, k), lambda i, j: (i, 0)),
            pl.BlockSpec((k, d), lambda i, j: (0, 0)),
            pl.BlockSpec((tm, d), lambda i, j: (i, 0)),
        ],
        out_specs=pl.BlockSpec((tm, d), lambda i, j: (i, 0)),
        out_shape=jax.ShapeDtypeStruct((t, d), F32),
        compiler_params=_COMPILER_PARAMS,
        name="attn_out_proj",
    )(a, w, h)


def _rope_tables(first_pos, n_pos):
    pos = jnp.arange(first_pos, first_pos + n_pos, dtype=F32)
    inv = ROPE_THETA ** (-jnp.arange(0, HEAD_DIM, 2, dtype=F32) / HEAD_DIM)
    inv = jnp.concatenate([inv, inv, inv, inv])
    sign = jnp.concatenate([-jnp.ones_like(inv[:HALF_DIM]), jnp.ones_like(inv[:HALF_DIM])] * 2)
    ang = pos[:, None] * inv[None, :]
    return jnp.cos(ang), jnp.sin(ang) * sign


def _scale_q_block(blk, *, d):
    return jnp.concatenate([blk[:, :d] * HEAD_DIM ** -0.5, blk[:, d:]], axis=1).astype(BF16)


def kernel(x, meta_tokens, norm_mix_0, w_in_conv, conv_w, w_out_conv, norm_mlp_0, w_up_0, w_down_0,
           norm_mix_1, w_qkv, attn_sinks, w_o, norm_mlp_1, w_up_1, w_down_1, norm_final):
    batch, seq, d = x.shape
    n_meta = meta_tokens.shape[0]
    n_kv = (w_qkv.shape[1] - d) // (2 * HEAD_DIM)
    pad = (-(seq + n_meta)) % ATTN_BLOCK
    assert seq % ATTN_BLOCK == 0 and pad + n_meta == ATTN_BLOCK
    assert n_meta % (2 * V7X_SUBLANES) == 0 and d // HEAD_DIM == 8 * n_kv

    tm, cc, nc, fc = TOKEN_TILE, CONV_CHUNK, COL_CHUNK, FF_CHUNK
    assert seq % tm == 0 and d % nc == 0
    assert 2 * n_kv * HEAD_DIM == nc, "k and v heads together must fill exactly one column chunk"
    tiles_per_seq = seq // tm

    row = lambda g: g.reshape(1, d)
    g_mix0, g_mlp0, g_mix1, g_mlp1, g_fin = map(row, (norm_mix_0, norm_mlp_0, norm_mix_1, norm_mlp_1, norm_final))
    cos_m, sin_m = _rope_tables(0, n_meta)
    cos_t, sin_t = _rope_tables(n_meta, seq)
    zeros_tail = jnp.zeros((V7X_SUBLANES, d), F32)
    hm, tail_m, wb, wc, wu, w_out = _conv_mixer(
        meta_tokens, g_mix0, [(w_in_conv, 0), (w_in_conv, d), (w_in_conv, 2 * d)], conv_w, w_out_conv,
        zeros_tail, tm=n_meta, nc=META_CONV_CHUNK, tiles_per_seq=1, cast_weights=True)
    h = x.reshape(batch * seq, d)
    qkv_cast = _SideCast(w_qkv, functools.partial(_scale_q_block, d=d))
    h, _, wu0, wd0, w_ext, wo, wu1, wd1 = _conv_mixer(
        h, g_mix0, [(wb, 0), (wc, 0), (wu, 0)], conv_w, w_out, tail_m[-V7X_SUBLANES:],
        tm=tm, nc=cc, tiles_per_seq=tiles_per_seq,
        side_casts=(_plain_cast(w_up_0), _plain_cast(w_down_0), qkv_cast, _plain_cast(w_o),
                    _plain_cast(w_up_1), _plain_cast(w_down_1)))

    hm = _mlp(hm, g_mlp0, wu0, wd0, g_fin, tm=n_meta, fc=fc, final_norm=False)
    h = _mlp(h, g_mlp0, wu0, wd0, g_fin, tm=tm, fc=fc, final_norm=False)

    _, kv_m = _qkv(hm, g_mix1, w_ext, cos_m, sin_m, tm=n_meta, nc=nc, tiles_per_seq=1)
    kv_meta = jnp.pad(kv_m, ((pad, 0), (0, 0)))
    q, kv = _qkv(h, g_mix1, w_ext, cos_t, sin_t, tm=tm, nc=nc, tiles_per_seq=tiles_per_seq)
    a = _attention(attn_sinks, q, kv, kv_meta, batch=batch, n_kv=n_kv, pad=pad)
    h = _out_proj(a, wo, h, tm=OUT_PROJ_TILE)
    out = _mlp(h, g_mlp1, wu1, wd1, g_fin, tm=tm, fc=fc, final_norm=True)
    return out.reshape(batch, seq, d)
```

```python
import functools
from typing import Callable, NamedTuple

import jax
import jax.numpy as jnp
from jax import lax
from jax.experimental import pallas as pl
from jax.experimental.pallas import tpu as pltpu

HEAD_DIM = 64
HALF_DIM = HEAD_DIM // 2
ATTN_BLOCK = 128
ROPE_THETA = 10000.0
RMS_EPS = 1e-5
NEG_INF = -1e30
CONV_WIDTH = 3

V7X_LANES = 128
V7X_SUBLANES = 8
V7X_BF16_ROWS = 16
V7X_VMEM_LIMIT_BYTES = 60000 * 1024

TOKEN_TILE = 1024
OUT_PROJ_TILE = 512
CONV_CHUNK = 512
META_CONV_CHUNK = 256
COL_CHUNK = 512
FF_CHUNK = 2048
QKV_ROW_SPLITS = 4
MLP_ROW_SPLITS = 4
CONV_NORM_ROW_SPLITS = 2

BF16 = jnp.bfloat16
F32 = jnp.float32

_COMPILER_PARAMS = pltpu.CompilerParams(
    dimension_semantics=("arbitrary", "arbitrary"), vmem_limit_bytes=V7X_VMEM_LIMIT_BYTES)


def _dot(a, b):
    return jnp.dot(a, b, preferred_element_type=F32)


def _rms_norm(x, g):
    var = jnp.mean(x * x, axis=-1, keepdims=True)
    return (x * lax.rsqrt(var + RMS_EPS)) * g


def _row_splits(tm, splits):
    return splits if tm % (splits * 128) == 0 else 1


class _SideCast(NamedTuple):
    w: jax.Array
    fn: Callable


def _plain_cast(w):
    return _SideCast(w, lambda blk: blk.astype(BF16))


def _side_cast_specs(sides, nt, nj):
    n_steps = nt * nj
    in_specs, out_specs, out_shapes = [], [], []
    for s in sides:
        rows, cols = s.w.shape
        n_blocks = 1
        while 2 * n_blocks <= min(n_steps, rows // V7X_BF16_ROWS) and rows % (2 * n_blocks) == 0:
            n_blocks *= 2
        rb = rows // n_blocks
        assert rb % V7X_BF16_ROWS == 0
        index = lambda i, j, nb=n_blocks: (((i * nj + j) * nb) // n_steps, 0)
        in_specs.append(pl.BlockSpec((rb, cols), index))
        out_specs.append(pl.BlockSpec((rb, cols), index))
        out_shapes.append(jax.ShapeDtypeStruct((rows, cols), BF16))
    return in_specs, out_specs, out_shapes


def _run_side_casts(fns, in_refs, out_refs):
    for fn, w_ref, o_ref in zip(fns, in_refs, out_refs):
        o_ref[...] = fn(w_ref[...])


class _TileStream:
    def __init__(self, x_hbm, o_hbm, buf, in_sem, out_sem):
        self.x_hbm, self.o_hbm, self.buf, self.in_sem, self.out_sem = x_hbm, o_hbm, buf, in_sem, out_sem
        self.i, self.j = pl.program_id(0), pl.program_id(1)
        self.nt = pl.num_programs(0)
        self.tm = buf.shape[1]
        self.slot = self.i % 2
        self.tile = buf.at[self.slot]

    def _rows(self, tile):
        return pl.ds(pl.multiple_of(tile * self.tm, self.tm), self.tm)

    def _arrival(self, tile, slot):
        return pltpu.make_async_copy(self.x_hbm.at[self._rows(tile), :], self.buf.at[slot], self.in_sem.at[slot])

    def _departure(self, tile, slot):
        return pltpu.make_async_copy(self.buf.at[slot], self.o_hbm.at[self._rows(tile), :], self.out_sem.at[slot])

    def begin_step(self, n_steps):
        i, j, other = self.i, self.j, 1 - self.slot

        @pl.when(j == 0)
        def _():
            @pl.when(i == 0)
            def _():
                self._arrival(0, 0).start()
            self._arrival(i, self.slot).wait()

        @pl.when(j == min(1, n_steps - 1))
        def _():
            @pl.when(i > 0)
            def _():
                self._departure(i - 1, other).wait()

            @pl.when(i + 1 < self.nt)
            def _():
                self._arrival(i + 1, other).start()

    def end_step(self, n_steps):
        i = self.i

        @pl.when(self.j == n_steps - 1)
        def _():
            self._departure(i, self.slot).start()

            @pl.when(i == self.nt - 1)
            def _():
                self._departure(i, self.slot).wait()


def _conv_mixer_kernel(*refs, tiles_per_seq, n_steps, side_fns, cast_weights):
    n = len(side_fns)
    n_w = 4 if cast_weights else 0
    x_hbm, g_ref, wb_ref, wc_ref, wu_ref, cw_ref, wo_ref, tail_in_ref = refs[:8]
    o_hbm, tail_out_ref = refs[8 + n:10 + n]
    w_bf16_refs = refs[10 + 2 * n:10 + 2 * n + n_w]
    hn_ref, carry_ref, buf, in_sem, out_sem = refs[10 + 2 * n + n_w:]
    stream = _TileStream(x_hbm, o_hbm, buf, in_sem, out_sem)
    stream.begin_step(n_steps)
    o_ref = stream.tile
    i = pl.program_id(0)
    j = pl.program_id(1)
    tm = hn_ref.shape[0]
    cw = cw_ref[...]
    first_tile_of_seq = (i % tiles_per_seq) == 0
    tile_prev = jnp.where(first_tile_of_seq, tail_in_ref[...], carry_ref[j])

    weights = (wb_ref, wc_ref, wu_ref, wo_ref)
    if cast_weights:
        weights = tuple(w[...].astype(BF16) for w in weights)
        for w_out_ref, w in zip(w_bf16_refs, weights):
            w_out_ref[...] = w
    wb, wc, wu, wo = weights

    def mix(hn, prev):
        gate_b = _dot(hn, wb[...])
        v = _dot(hn, wc[...]) * _dot(hn, wu[...])
        vcat = jnp.concatenate([prev, v], axis=0)
        v_m1 = pltpu.roll(vcat, 1, axis=0)[V7X_SUBLANES:]
        v_m2 = pltpu.roll(vcat, 2, axis=0)[V7X_SUBLANES:]
        conv = cw[0:1] * v_m2 + cw[1:2] * v_m1 + cw[2:3] * v
        gated = (gate_b * conv).astype(BF16)
        return _dot(gated, wo[...]), v[v.shape[0] - V7X_SUBLANES:]

    def finish(tail):
        carry_ref[j] = tail
        tail_out_ref[...] = tail
        _run_side_casts(side_fns, refs[8:8 + n], refs[10 + n:10 + 2 * n])

    @pl.when(j == 0)
    def _():
        rm = tm // _row_splits(tm, CONV_NORM_ROW_SPLITS)
        prev = tile_prev
        for r0 in range(0, tm, rm):
            rows = slice(r0, r0 + rm)
            x = o_ref[rows, :]
            hn = _rms_norm(x, g_ref[...]).astype(BF16)
            hn_ref[rows, :] = hn
            y, prev = mix(hn, prev)
            o_ref[rows, :] = x + y
        finish(prev)

    @pl.when(j > 0)
    def _():
        y, tail = mix(hn_ref[...], tile_prev)
        o_ref[...] += y
        finish(tail)

    stream.end_step(n_steps)


def _conv_mixer(x, gain, w_in_parts, conv_w, w_out, tail_in, *, tm, nc, tiles_per_seq, side_casts=(),
                cast_weights=False):
    t, d = x.shape
    nt, nj = t // tm, d // nc
    side_in, side_out, side_shapes = _side_cast_specs(side_casts, nt, nj)
    kern = functools.partial(_conv_mixer_kernel, tiles_per_seq=tiles_per_seq, n_steps=nj,
                             side_fns=tuple(s.fn for s in side_casts), cast_weights=cast_weights)
    in_proj_specs = [pl.BlockSpec((d, nc), lambda i, j, c0=col0 // nc: (0, c0 + j)) for _, col0 in w_in_parts]
    w_copy_specs, w_copy_shapes = [], []
    if cast_weights:
        w_copy_specs = [pl.BlockSpec((d, nc), lambda i, j: (0, j))] * 3 + [pl.BlockSpec((nc, d), lambda i, j: (j, 0))]
        w_copy_shapes = [jax.ShapeDtypeStruct((d, d), BF16)] * 4
    return pl.pallas_call(
        kern,
        grid=(nt, nj),
        in_specs=[
            pl.BlockSpec(memory_space=pl.ANY),
            pl.BlockSpec((1, d), lambda i, j: (0, 0)),
            *in_proj_specs,
            pl.BlockSpec((CONV_WIDTH, nc), lambda i, j: (0, j)),
            pl.BlockSpec((nc, d), lambda i, j: (j, 0)),
            pl.BlockSpec((V7X_SUBLANES, nc), lambda i, j: (0, j)),
        ] + side_in,
        out_specs=[
            pl.BlockSpec(memory_space=pl.ANY),
            pl.BlockSpec((V7X_SUBLANES, nc), lambda i, j: (i, j)),
        ] + side_out + w_copy_specs,
        out_shape=[
            jax.ShapeDtypeStruct((t, d), F32),
            jax.ShapeDtypeStruct((nt * V7X_SUBLANES, d), F32),
        ] + side_shapes + w_copy_shapes,
        scratch_shapes=[
            pltpu.VMEM((tm, d), BF16),
            pltpu.VMEM((nj, V7X_SUBLANES, nc), F32),
            pltpu.VMEM((2, tm, d), F32),
            pltpu.SemaphoreType.DMA((2,)),
            pltpu.SemaphoreType.DMA((2,)),
        ],
        compiler_params=_COMPILER_PARAMS,
        name="conv_mixer",
    )(x, gain, *(w for w, _ in w_in_parts), conv_w, w_out, tail_in, *(s.w for s in side_casts))


def _mlp_kernel(x_hbm, g_ref, wu_ref, wd_ref, gf_ref, o_hbm, hn_ref, buf, in_sem, out_sem, *,
                final_norm, n_steps):
    stream = _TileStream(x_hbm, o_hbm, buf, in_sem, out_sem)
    stream.begin_step(n_steps)
    o_ref = stream.tile
    j = pl.program_id(1)
    tm = hn_ref.shape[0]
    rm = tm // _row_splits(tm, MLP_ROW_SPLITS)

    def ffn(hn):
        a = jnp.maximum(_dot(hn, wu_ref[...]), 0.0)
        return _dot((a * a).astype(BF16), wd_ref[...])

    @pl.when(j == 0)
    def _():
        for r0 in range(0, tm, rm):
            rows = slice(r0, r0 + rm)
            x = o_ref[rows, :]
            hn = _rms_norm(x, g_ref[...]).astype(BF16)
            hn_ref[rows, :] = hn
            o_ref[rows, :] = x + ffn(hn)

    @pl.when(j > 0)
    def _():
        for r0 in range(0, tm, rm):
            rows = slice(r0, r0 + rm)
            o_ref[rows, :] += ffn(hn_ref[rows, :])

    if final_norm:
        @pl.when(j == n_steps - 1)
        def _():
            o_ref[...] = _rms_norm(o_ref[...], gf_ref[...])

    stream.end_step(n_steps)


def _mlp(x, gain, w_up, w_down, gain_final, *, tm, fc, final_norm):
    t, d = x.shape
    f = w_up.shape[1]
    fc = min(fc, f)
    n_steps = f // fc
    return pl.pallas_call(
        functools.partial(_mlp_kernel, final_norm=final_norm, n_steps=n_steps),
        grid=(t // tm, n_steps),
        in_specs=[
            pl.BlockSpec(memory_space=pl.ANY),
            pl.BlockSpec((1, d), lambda i, j: (0, 0)),
            pl.BlockSpec((d, fc), lambda i, j: (0, j)),
            pl.BlockSpec((fc, d), lambda i, j: (j, 0)),
            pl.BlockSpec((1, d), lambda i, j: (0, 0)),
        ],
        out_specs=pl.BlockSpec(memory_space=pl.ANY),
        out_shape=jax.ShapeDtypeStruct((t, d), F32),
        scratch_shapes=[
            pltpu.VMEM((tm, d), BF16),
            pltpu.VMEM((2, tm, d), F32),
            pltpu.SemaphoreType.DMA((2,)),
            pltpu.SemaphoreType.DMA((2,)),
        ],
        compiler_params=_COMPILER_PARAMS,
        name="mlp_final" if final_norm else "mlp",
    )(x, gain, w_up, w_down, gain_final)


def _qkv_kernel(x_ref, g_ref, w_ref, cos_ref, sin_ref, q_ref, kv_ref, *, nc):
    lane = lax.broadcasted_iota(jnp.int32, (1, V7X_LANES), 1)
    first_half = (lane % HEAD_DIM) < HALF_DIM
    tm, d = x_ref.shape
    slabs = nc // V7X_LANES
    half = slabs // 2
    rm = tm // _row_splits(tm, QKV_ROW_SPLITS)

    def rope(t, rows):
        rot = jnp.where(first_half,
                        pltpu.roll(t, V7X_LANES - HALF_DIM, axis=1),
                        pltpu.roll(t, HALF_DIM, axis=1))
        return t * cos_ref[rows, :] + rot * sin_ref[rows, :]

    for r0 in range(0, tm, rm):
        rows = slice(r0, r0 + rm)
        hn = _rms_norm(x_ref[rows, :], g_ref[...]).astype(BF16)
        for c0 in range(0, d, nc):
            y = _dot(hn, w_ref[:, c0:c0 + nc])
            for s in range(slabs):
                sl = slice(s * V7X_LANES, (s + 1) * V7X_LANES)
                q_ref[rows, c0 + s * V7X_LANES:c0 + (s + 1) * V7X_LANES] = rope(y[:, sl], rows).astype(BF16)
        y = _dot(hn, w_ref[:, d:d + nc])
        for s in range(slabs):
            t = y[:, s * V7X_LANES:(s + 1) * V7X_LANES]
            if s < half:
                t = rope(t, rows)
            first = (s + (s // half) * half) * V7X_LANES
            kv_ref[rows, first:first + V7X_LANES] = t.astype(BF16)
            kv_ref[rows, first + half * V7X_LANES:first + (half + 1) * V7X_LANES] = (
                pltpu.roll(t, HEAD_DIM, axis=1).astype(BF16))


def _qkv(x, gain, w_ext, cos_t, sin_t, *, tm, nc, tiles_per_seq):
    t, d = x.shape
    nw = w_ext.shape[1]
    assert nw == d + nc and d % nc == 0
    return pl.pallas_call(
        functools.partial(_qkv_kernel, nc=nc),
        grid=(t // tm, 1),
        in_specs=[
            pl.BlockSpec((tm, d), lambda i, j: (i, 0)),
            pl.BlockSpec((1, d), lambda i, j: (0, 0)),
            pl.BlockSpec((d, nw), lambda i, j: (0, 0)),
            pl.BlockSpec((tm, V7X_LANES), lambda i, j: (i % tiles_per_seq, 0)),
            pl.BlockSpec((tm, V7X_LANES), lambda i, j: (i % tiles_per_seq, 0)),
        ],
        out_specs=[
            pl.BlockSpec((tm, d), lambda i, j: (i, 0)),
            pl.BlockSpec((tm, 2 * nc), lambda i, j: (i, 0)),
        ],
        out_shape=[jax.ShapeDtypeStruct((t, d), BF16), jax.ShapeDtypeStruct((t, 2 * nc), BF16)],
        compiler_params=_COMPILER_PARAMS,
        name="qkv_rope",
    )(x, gain, w_ext, cos_t, sin_t)


def _attn_kernel(sink_ref, q_ref, kvc_ref, kvp_ref, kvm_ref, o_ref, *, n_kv, pad):
    n = pl.program_id(1)
    blk = ATTN_BLOCK
    kv_prev = jnp.where(n == 0, kvm_ref[...], kvp_ref[...])
    band = jnp.concatenate([kv_prev, kvc_ref[...]], axis=0)

    qi = lax.broadcasted_iota(jnp.int32, (blk, 2 * blk), 0)
    kj = lax.broadcasted_iota(jnp.int32, (blk, 2 * blk), 1)
    allowed = (kj > qi) & (kj <= qi + blk) & (kj + n * blk >= pad)

    sink_slot = lax.broadcasted_iota(jnp.int32, (1, 2 * blk), 1) == 0
    low = lax.broadcasted_iota(jnp.int32, (2 * blk, V7X_LANES), 1) < HEAD_DIM
    not_sink_row = lax.broadcasted_iota(jnp.int32, (V7X_BF16_ROWS, V7X_LANES), 0) > 0
    zero = jnp.zeros((), BF16)
    kv_width = n_kv * HEAD_DIM

    def head_sides(first_col, h, clear_sink_row):
        def slab(col):
            x = band[:, col:col + V7X_LANES]
            if clear_sink_row:
                first = jnp.where(not_sink_row, x[:V7X_BF16_ROWS].astype(F32), 0.0).astype(BF16)
                x = jnp.concatenate([first, x[V7X_BF16_ROWS:]], axis=0)
            return x
        plain = slab(first_col + (h // 2) * V7X_LANES)
        swapped = slab(first_col + kv_width + (h // 2) * V7X_LANES)
        return (plain, swapped) if h % 2 == 0 else (swapped, plain)

    low_row = lax.broadcasted_iota(jnp.int32, (1, V7X_LANES), 1) < HEAD_DIM
    ones_low = jnp.broadcast_to(jnp.where(low_row, 1.0, 0.0).astype(BF16), (2 * blk, V7X_LANES))
    ones_high = jnp.broadcast_to(jnp.where(low_row, 0.0, 1.0).astype(BF16), (2 * blk, V7X_LANES))
    pairs = 4
    contract_lanes = (((1,), (1,)), ((), ()))

    for h in range(n_kv):
        k_low, k_high = head_sides(0, h, clear_sink_row=False)
        v_low, v_high = head_sides(2 * kv_width, h, clear_sink_row=True)
        k_sides = (jnp.where(low, k_low, zero), jnp.where(low, zero, k_high))
        values = jnp.concatenate([
            jnp.concatenate([jnp.where(low, v_low, zero), ones_low], axis=1),
            jnp.concatenate([jnp.where(low, zero, v_high), ones_high], axis=1)], axis=0)
        qbase = h * pairs * V7X_LANES
        qst = jnp.concatenate(
            [q_ref[:, qbase + p * V7X_LANES: qbase + (p + 1) * V7X_LANES] for p in range(pairs)],
            axis=0)
        e_sides = []
        for side in range(2):
            s_all = lax.dot_general(qst, k_sides[side], contract_lanes, preferred_element_type=F32)
            es = []
            for p in range(pairs):
                fill = jnp.where(sink_slot, sink_ref[h * 2 * pairs + 2 * p + side], NEG_INF)
                s = jnp.where(allowed, s_all[p * blk:(p + 1) * blk], fill)
                e = jnp.exp(s - jnp.max(s, axis=-1, keepdims=True))
                es.append(e.astype(BF16))
            e_sides.append(jnp.concatenate(es, axis=0))
        o_den = _dot(jnp.concatenate(e_sides, axis=1), values)
        out = o_den[:, :V7X_LANES] * (1.0 / o_den[:, V7X_LANES:])
        for p in range(pairs):
            o_ref[:, qbase + p * V7X_LANES: qbase + (p + 1) * V7X_LANES] = (
                out[p * blk:(p + 1) * blk].astype(BF16))


def _attention(sinks, q, kv, kv_meta, *, batch, n_kv, pad):
    t, d = q.shape
    blk = ATTN_BLOCK
    nb = t // batch // blk
    kvw = kv.shape[1]
    assert kvw == 4 * n_kv * HEAD_DIM and kv_meta.shape == (blk, kvw)
    return pl.pallas_call(
        functools.partial(_attn_kernel, n_kv=n_kv, pad=pad),
        grid=(batch, nb),
        in_specs=[
            pl.BlockSpec(memory_space=pltpu.SMEM),
            pl.BlockSpec((blk, d), lambda b, n: (b * nb + n, 0)),
            pl.BlockSpec((blk, kvw), lambda b, n: (b * nb + n, 0)),
            pl.BlockSpec((blk, kvw), lambda b, n: (b * nb + jnp.maximum(n - 1, 0), 0)),
            pl.BlockSpec((blk, kvw), lambda b, n: (0, 0)),
        ],
        out_specs=pl.BlockSpec((blk, d), lambda b, n: (b * nb + n, 0)),
        out_shape=jax.ShapeDtypeStruct((t, d), BF16),
        compiler_params=_COMPILER_PARAMS,
        name="swa_attention",
    )(sinks, q, kv, kv, kv_meta)


def _out_proj_kernel(a_ref, w_ref, h_ref, o_ref):
    o_ref[...] = h_ref[...] + _dot(a_ref[...], w_ref[...])


def _out_proj(a, w, h, *, tm):
    t, d = h.shape
    k = a.shape[1]
    return pl.pallas_call(
        _out_proj_kernel,
        grid=(t // tm, 1),
        in_specs=[
            pl.BlockSpec((tm, k), lambda i, j: (i, 0)),
            pl.BlockSpec((k, d), lambda i, j: (0, 0)),
            pl.BlockSpec((tm, d), lambda i, j: (i, 0)),
        ],
        out_specs=pl.BlockSpec((tm, d), lambda i, j: (i, 0)),
        out_shape=jax.ShapeDtypeStruct((t, d), F32),
        compiler_params=_COMPILER_PARAMS,
        name="attn_out_proj",
    )(a, w, h)


def _rope_tables(first_pos, n_pos):
    pos = jnp.arange(first_pos, first_pos + n_pos, dtype=F32)
    inv = ROPE_THETA ** (-jnp.arange(0, HEAD_DIM, 2, dtype=F32) / HEAD_DIM)
    inv = jnp.concatenate([inv, inv, inv, inv])
    sign = jnp.concatenate([-jnp.ones_like(inv[:HALF_DIM]), jnp.ones_like(inv[:HALF_DIM])] * 2)
    ang = pos[:, None] * inv[None, :]
    return jnp.cos(ang), jnp.sin(ang) * sign


def _scale_q_block(blk, *, d):
    return jnp.concatenate([blk[:, :d] * HEAD_DIM ** -0.5, blk[:, d:]], axis=1).astype(BF16)


def kernel(x, meta_tokens, norm_mix_0, w_in_conv, conv_w, w_out_conv, norm_mlp_0, w_up_0, w_down_0,
           norm_mix_1, w_qkv, attn_sinks, w_o, norm_mlp_1, w_up_1, w_down_1, norm_final):
    batch, seq, d = x.shape
    n_meta = meta_tokens.shape[0]
    n_kv = (w_qkv.shape[1] - d) // (2 * HEAD_DIM)
    pad = (-(seq + n_meta)) % ATTN_BLOCK
    assert seq % ATTN_BLOCK == 0 and pad + n_meta == ATTN_BLOCK
    assert n_meta % (2 * V7X_SUBLANES) == 0 and d // HEAD_DIM == 8 * n_kv

    tm, cc, nc, fc = TOKEN_TILE, CONV_CHUNK, COL_CHUNK, FF_CHUNK
    assert seq % tm == 0 and d % nc == 0
    assert 2 * n_kv * HEAD_DIM == nc, "k and v heads together must fill exactly one column chunk"
    tiles_per_seq = seq // tm

    row = lambda g: g.reshape(1, d)
    g_mix0, g_mlp0, g_mix1, g_mlp1, g_fin = map(row, (norm_mix_0, norm_mlp_0, norm_mix_1, norm_mlp_1, norm_final))
    cos_m, sin_m = _rope_tables(0, n_meta)
    cos_t, sin_t = _rope_tables(n_meta, seq)
    zeros_tail = jnp.zeros((V7X_SUBLANES, d), F32)
    hm, tail_m, wb, wc, wu, w_out = _conv_mixer(
        meta_tokens, g_mix0, [(w_in_conv, 0), (w_in_conv, d), (w_in_conv, 2 * d)], conv_w, w_out_conv,
        zeros_tail, tm=n_meta, nc=META_CONV_CHUNK, tiles_per_seq=1, cast_weights=True)
    h = x.reshape(batch * seq, d)
    qkv_cast = _SideCast(w_qkv, functools.partial(_scale_q_block, d=d))
    h, _, wu0, wd0, w_ext, wo, wu1, wd1 = _conv_mixer(
        h, g_mix0, [(wb, 0), (wc, 0), (wu, 0)], conv_w, w_out, tail_m[-V7X_SUBLANES:],
        tm=tm, nc=cc, tiles_per_seq=tiles_per_seq,
        side_casts=(_plain_cast(w_up_0), _plain_cast(w_down_0), qkv_cast, _plain_cast(w_o),
                    _plain_cast(w_up_1), _plain_cast(w_down_1)))

    hm = _mlp(hm, g_mlp0, wu0, wd0, g_fin, tm=n_meta, fc=fc, final_norm=False)
    h = _mlp(h, g_mlp0, wu0, wd0, g_fin, tm=tm, fc=fc, final_norm=False)

    _, kv_m = _qkv(hm, g_mix1, w_ext, cos_m, sin_m, tm=n_meta, nc=nc, tiles_per_seq=1)
    kv_meta = jnp.pad(kv_m, ((pad, 0), (0, 0)))
    q, kv = _qkv(h, g_mix1, w_ext, cos_t, sin_t, tm=tm, nc=nc, tiles_per_seq=tiles_per_seq)
    a = _attention(attn_sinks, q, kv, kv_meta, batch=batch, n_kv=n_kv, pad=pad)
    h = _out_proj(a, wo, h, tm=OUT_PROJ_TILE)
    out = _mlp(h, g_mlp1, wu1, wd1, g_fin, tm=tm, fc=fc, final_norm=True)
    return out.reshape(batch, seq, d)
```
